```python
import math
import jax, jax.numpy as jnp
from jax import lax
import numpy as np

D_MODEL = 1024
BATCH = 16
SEQ = 2048
DEPTH = 2

HEAD_DIM = 64
ATTN_WIDTH = D_MODEL // 2
ATTN_HEADS = ATTN_WIDTH // HEAD_DIM
DILATED_CONFIGS = ((128, 1), (512, 4), (2048, 16))
ROPE_THETA = 500000.0
ROPE_DIM = HEAD_DIM // 4
GMLP_WIDTH = D_MODEL // 4
GMLP_GROUPS = 4
GMLP_GROUP_DIM = GMLP_WIDTH // GMLP_GROUPS
GMLP_CHUNK = 128
POOL_WIDTH = D_MODEL - ATTN_WIDTH - GMLP_WIDTH
POOL_WINDOWS = (2, 4, 8, 16)
POOL_GROUPS = len(POOL_WINDOWS)
POOL_GROUP_DIM = POOL_WIDTH // POOL_GROUPS
IN_COLS = 3 * ATTN_WIDTH + 2 * GMLP_WIDTH + POOL_WIDTH

N_EXPERTS = 16
N_EXPERT_GROUPS = 4
EXPERTS_PER_GROUP = N_EXPERTS // N_EXPERT_GROUPS
TOP_K = 2
EXPERT_FF = D_MODEL // 2
MOE_BLOCK = 128

DEEPNORM_ALPHA = float((2 * DEPTH) ** 0.25)
DEEPNORM_BETA = float((8 * DEPTH) ** -0.25)
LN_EPS = 1e-5
NEG_INF = -1e30

kernel_name = "hybrid_dilated_gmlp_pool_groupmoe_deepnorm"


def _layernorm(x, g, b):
    xf = x.astype(jnp.float32)
    mu = xf.mean(-1, keepdims=True)
    var = jnp.square(xf - mu).mean(-1, keepdims=True)
    y = (xf - mu) * lax.rsqrt(var + LN_EPS) * g.astype(jnp.float32) + b.astype(jnp.float32)
    return y.astype(x.dtype)


def _partial_rope(x, positions):
    half = ROPE_DIM // 2
    inv_freq = jnp.power(jnp.float32(ROPE_THETA), -jnp.arange(half, dtype=jnp.float32) / half)
    ang = positions.astype(jnp.float32)[:, None] * inv_freq[None, :]
    cos = jnp.cos(ang)[None, :, None, :]
    sin = jnp.sin(ang)[None, :, None, :]
    xr = x[..., :ROPE_DIM].astype(jnp.float32)
    x1, x2 = xr[..., :half], xr[..., half:]
    rot = jnp.concatenate([x1 * cos - x2 * sin, x2 * cos + x1 * sin], axis=-1).astype(x.dtype)
    return jnp.concatenate([rot, x[..., ROPE_DIM:]], axis=-1)


def _banded_window_attention(q, k, v, half):
    lead = q.shape[:-2]
    L, dh = q.shape[-2], q.shape[-1]
    nlead = len(lead)
    blk = half
    nb = -(-L // blk)
    lp = nb * blk
    qb = jnp.pad(q, [(0, 0)] * nlead + [(0, lp - L), (0, 0)]).reshape(*lead, nb, blk, dh)
    kv_pad = [(0, 0)] * nlead + [(blk, lp - L + blk), (0, 0)]

    def band(t):
        tb = jnp.pad(t, kv_pad).reshape(*lead, nb + 2, blk, dh)
        return jnp.concatenate([tb[..., :-2, :, :], tb[..., 1:-1, :, :], tb[..., 2:, :, :]], axis=-2)

    kw = band(k)
    vw = band(v)
    s = jnp.einsum('...nqd,...nkd->...nqk', qb, kw,
                   preferred_element_type=jnp.float32) * (HEAD_DIM ** -0.5)
    a = jnp.arange(blk)[:, None]
    c = jnp.arange(3 * blk)[None, :]
    n = jnp.arange(nb)[:, None, None]
    offset = c - blk - a
    key_idx = (n - 1) * blk + c
    mask = (jnp.abs(offset) <= half) & (key_idx >= 0) & (key_idx < L)
    s = jnp.where(mask, s, NEG_INF)
    m = s.max(-1, keepdims=True)
    p = jnp.exp(s - m)
    l = p.sum(-1, keepdims=True)
    o = jnp.einsum('...nqk,...nkd->...nqd', p.astype(v.dtype), vw,
                   preferred_element_type=jnp.float32) / l
    lse = (m + jnp.log(l))[..., 0]
    o = o.reshape(*lead, lp, dh)[..., :L, :]
    lse = lse.reshape(*lead, lp)[..., :L]
    return o, lse


def _dilated_attention(q, k, v):
    B, H, S, dh = q.shape
    outs, lses = [], []
    for window, dil in DILATED_CONFIGS:
        half = window // (2 * dil)

        def to_residue(t):
            return t.reshape(B, H, S // dil, dil, dh).swapaxes(2, 3)

        o, lse = _banded_window_attention(to_residue(q), to_residue(k), to_residue(v), half)
        outs.append(o.swapaxes(2, 3).reshape(B, H, S, dh))
        lses.append(lse.swapaxes(2, 3).reshape(B, H, S))
    wts = jax.nn.softmax(jnp.stack(lses, axis=0), axis=0)
    out = jnp.einsum('cbhs,cbhsd->bhsd', wts, jnp.stack(outs, axis=0))
    return out.astype(q.dtype)


def _gmlp_mixer(u, v, ln_g, ln_b, w_s, b_s):
    B, S, G, dg = u.shape
    u = jax.nn.gelu(u)
    v = _layernorm(jax.nn.gelu(v), ln_g, ln_b)
    vc = v.reshape(B, S // GMLP_CHUNK, GMLP_CHUNK, G, dg)
    sv = jnp.einsum('gpq,bcqgd->bcpgd', w_s, vc) + b_s.T[None, None, :, :, None]
    return u * sv.reshape(B, S, G, dg)


def _pool_mixer(z, w_pool, scale):
    B, S, G, dg = z.shape
    zf = z.astype(jnp.float32)
    cs = jnp.concatenate([jnp.zeros((B, 1, G, dg), jnp.float32), jnp.cumsum(zf, axis=1)], axis=1)
    pos = jnp.arange(S)
    pooled = []
    for g, w in enumerate(POOL_WINDOWS):
        left = w // 2
        right = w - 1 - left
        lo = jnp.clip(pos - left, 0, S)
        hi = jnp.clip(pos + right + 1, 0, S)
        cnt = (hi - lo).astype(jnp.float32)[None, :, None]
        csg = cs[:, :, g]
        mean = (jnp.take(csg, hi, axis=1) - jnp.take(csg, lo, axis=1)) / cnt
        pooled.append(mean - zf[:, :, g])
    pooled = jnp.stack(pooled, axis=2).astype(z.dtype)
    y = jnp.einsum('bsgi,gio->bsgo', pooled, w_pool)
    return y * scale.reshape(G, dg)


def _mixing_sublayer(x, w_in, w_out, gmlp_ln_g, gmlp_ln_b, gmlp_w_s, gmlp_b_s, pool_w, pool_scale):
    B, S, _ = x.shape
    proj = x @ w_in
    o1 = ATTN_WIDTH
    o2 = 2 * ATTN_WIDTH
    o3 = 3 * ATTN_WIDTH
    o4 = o3 + GMLP_WIDTH
    o5 = o4 + GMLP_WIDTH
    positions = jnp.arange(S)
    q = _partial_rope(proj[..., :o1].reshape(B, S, ATTN_HEADS, HEAD_DIM), positions)
    k = _partial_rope(proj[..., o1:o2].reshape(B, S, ATTN_HEADS, HEAD_DIM), positions)
    v = proj[..., o2:o3].reshape(B, S, ATTN_HEADS, HEAD_DIM)
    attn = _dilated_attention(q.transpose(0, 2, 1, 3), k.transpose(0, 2, 1, 3), v.transpose(0, 2, 1, 3))
    attn = attn.transpose(0, 2, 1, 3).reshape(B, S, ATTN_WIDTH)
    u = proj[..., o3:o4].reshape(B, S, GMLP_GROUPS, GMLP_GROUP_DIM)
    vg = proj[..., o4:o5].reshape(B, S, GMLP_GROUPS, GMLP_GROUP_DIM)
    gm = _gmlp_mixer(u, vg, gmlp_ln_g, gmlp_ln_b, gmlp_w_s, gmlp_b_s).reshape(B, S, GMLP_WIDTH)
    pz = proj[..., o5:].reshape(B, S, POOL_GROUPS, POOL_GROUP_DIM)
    pl = _pool_mixer(pz, pool_w, pool_scale).reshape(B, S, POOL_WIDTH)
    return jnp.concatenate([attn, gm, pl], axis=-1) @ w_out


def _route(x_flat, router_w, router_bias):
    T = x_flat.shape[0]
    scores = jax.nn.sigmoid((x_flat @ router_w).astype(jnp.float32))
    biased = (scores + router_bias.astype(jnp.float32)).reshape(T, N_EXPERT_GROUPS, EXPERTS_PER_GROUP)
    group_score = lax.top_k(biased, TOP_K)[0].sum(-1)
    gsel = jnp.argmax(group_score, axis=-1)
    in_group = biased[jnp.arange(T), gsel]
    _, local = lax.top_k(in_group, TOP_K)
    experts = gsel[:, None] * EXPERTS_PER_GROUP + local
    sel = jnp.take_along_axis(scores, experts, axis=1)
    gates = sel / sel.sum(-1, keepdims=True)
    return experts, gates


def _moe(x, router_w, router_bias, w_gate, w_up, w_down):
    B, S, D = x.shape
    T = B * S
    A = T * TOP_K
    xf = x.reshape(T, D)
    experts, gates = _route(xf, router_w, router_bias)
    e_flat = experts.reshape(A)
    tok_flat = jnp.broadcast_to(jnp.arange(T)[:, None], (T, TOP_K)).reshape(A)
    g_flat = gates.reshape(A)
    order = jnp.argsort(e_flat)
    e_sorted = e_flat[order]
    tok_sorted = tok_flat[order]
    g_sorted = g_flat[order]
    counts = jnp.zeros((N_EXPERTS,), jnp.int32).at[e_flat].add(1)
    padded = ((counts + MOE_BLOCK - 1) // MOE_BLOCK) * MOE_BLOCK
    start = jnp.cumsum(counts) - counts
    pend = jnp.cumsum(padded)
    pstart = pend - padded
    dest = pstart[e_sorted] + (jnp.arange(A) - start[e_sorted])
    P = A + N_EXPERTS * MOE_BLOCK
    n_blocks = P // MOE_BLOCK
    buf = jnp.zeros((P, D), x.dtype).at[dest].set(xf[tok_sorted])
    blk_expert = jnp.minimum(
        jnp.searchsorted(pend, jnp.arange(n_blocks) * MOE_BLOCK, side='right'), N_EXPERTS - 1)

    def expert_block(args):
        xb, e = args
        h = jax.nn.silu(xb @ w_gate[e]) * (xb @ w_up[e])
        return h @ w_down[e]

    y_buf = lax.map(expert_block, (buf.reshape(n_blocks, MOE_BLOCK, D), blk_expert))
    y_sorted = y_buf.reshape(P, D)[dest] * g_sorted[:, None].astype(x.dtype)
    out = jnp.zeros((T, D), x.dtype).at[tok_sorted].add(y_sorted)
    return out.reshape(B, S, D)


def setup_inputs(seed: int = 0) -> dict:
    key = jax.random.key(seed)
    ks = jax.random.split(key, 20)
    f32 = jnp.float32
    nrm = lambda k, shape, s: jax.random.normal(k, shape, f32) * s
    return {
        "x": nrm(ks[0], (BATCH, SEQ, D_MODEL), 1.0),
        "w_in": nrm(ks[1], (DEPTH, D_MODEL, IN_COLS), D_MODEL ** -0.5),
        "w_out": nrm(ks[2], (DEPTH, D_MODEL, D_MODEL), DEEPNORM_BETA * D_MODEL ** -0.5),
        "gmlp_ln_g": 1.0 + nrm(ks[3], (DEPTH, GMLP_GROUPS, GMLP_GROUP_DIM), 0.02),
        "gmlp_ln_b": nrm(ks[4], (DEPTH, GMLP_GROUPS, GMLP_GROUP_DIM), 0.02),
        "gmlp_w_s": nrm(ks[5], (DEPTH, GMLP_GROUPS, GMLP_CHUNK, GMLP_CHUNK), GMLP_CHUNK ** -0.5),
        "gmlp_b_s": 1.0 + nrm(ks[6], (DEPTH, GMLP_GROUPS, GMLP_CHUNK), 0.02),
        "pool_w": nrm(ks[7], (DEPTH, POOL_GROUPS, POOL_GROUP_DIM, POOL_GROUP_DIM), POOL_GROUP_DIM ** -0.5),
        "pool_scale": 1.0 + nrm(ks[8], (DEPTH, POOL_WIDTH), 0.02),
        "ln1_g": 1.0 + nrm(ks[9], (DEPTH, D_MODEL), 0.02),
        "ln1_b": nrm(ks[10], (DEPTH, D_MODEL), 0.02),
        "router_w": nrm(ks[11], (D_MODEL, N_EXPERTS), D_MODEL ** -0.5),
        "router_bias": nrm(ks[12], (N_EXPERTS,), 0.01),
        "w_gate": nrm(ks[13], (DEPTH, N_EXPERTS, D_MODEL, EXPERT_FF), D_MODEL ** -0.5),
        "w_up": nrm(ks[14], (DEPTH, N_EXPERTS, D_MODEL, EXPERT_FF), D_MODEL ** -0.5),
        "w_down": nrm(ks[15], (DEPTH, N_EXPERTS, EXPERT_FF, D_MODEL), DEEPNORM_BETA * EXPERT_FF ** -0.5),
        "ln2_g": 1.0 + nrm(ks[16], (DEPTH, D_MODEL), 0.02),
        "ln2_b": nrm(ks[17], (DEPTH, D_MODEL), 0.02),
    }


def reference(x, w_in, w_out, gmlp_ln_g, gmlp_ln_b, gmlp_w_s, gmlp_b_s, pool_w, pool_scale,
              ln1_g, ln1_b, router_w, router_bias, w_gate, w_up, w_down, ln2_g, ln2_b):
    for l in range(DEPTH):
        h = _mixing_sublayer(x, w_in[l], w_out[l], gmlp_ln_g[l], gmlp_ln_b[l], gmlp_w_s[l],
                             gmlp_b_s[l], pool_w[l], pool_scale[l])
        x = _layernorm(DEEPNORM_ALPHA * x + h, ln1_g[l], ln1_b[l])
        h = _moe(x, router_w, router_bias, w_gate[l], w_up[l], w_down[l])
        x = _layernorm(DEEPNORM_ALPHA * x + h, ln2_g[l], ln2_b[l])
    return x
```

```python
import functools

import jax
import jax.numpy as jnp
import numpy as np
from jax import lax
from jax.experimental import pallas as pl
from jax.experimental.pallas import tpu as pltpu

f32 = jnp.float32
bf16 = jnp.bfloat16

D_MODEL = 1024
HEAD_DIM = 64
ATTN_WIDTH = 512
DILATED_CONFIGS = ((128, 1), (512, 4), (2048, 16))
ATTN_HALF = 64
ROPE_THETA = 500000.0
ROPE_DIM = 16
GMLP_WIDTH = 256
GMLP_GROUPS = 4
GMLP_CHUNK = 128
POOL_WIDTH = 256
POOL_WINDOWS = (2, 4, 8, 16)
IN_COLS = 3 * ATTN_WIDTH + 2 * GMLP_WIDTH + POOL_WIDTH
N_EXPERTS = 16
N_EXPERT_GROUPS = 4
EXPERTS_PER_GROUP = 4
EXPERT_FF = 512
DEPTH = 2
DEEPNORM_ALPHA = float((2 * DEPTH) ** 0.25)
LN_EPS = 1e-5
NEG_INF = -1e30

LANES = 128
ROW_TILE = 512
ATTN_QBLK = 128
POOL_HALO = 32
N_PAIRS = 6
N_CLASSES = N_EXPERT_GROUPS * N_PAIRS
CLASS_ROWS = 32
MOE_BLK = 256
SIDE = LANES
PERM_CHUNK = 1024
VMEM_LIMIT = 48 * 1024 * 1024


def _layernorm(y, g, b):
    mu = jnp.mean(y, axis=-1, keepdims=True)
    d = y - mu
    var = jnp.mean(d * d, axis=-1, keepdims=True)
    return d * lax.rsqrt(var + LN_EPS) * g + b


def _group_mean(v):
    lane = lax.broadcasted_iota(jnp.int32, v.shape, 1)
    out = jnp.zeros_like(v)
    for g in range(GMLP_GROUPS):
        m = (lane >= g * HEAD_DIM) & (lane < (g + 1) * HEAD_DIM)
        s = jnp.sum(jnp.where(m, v, 0.0), axis=-1, keepdims=True)
        out = jnp.where(m, s, out)
    return out * (1.0 / HEAD_DIM)


def _inproj_kernel(x_ref, w_ref, cos_ref, sa_ref, sb_ref, lng_ref, lnb_ref,
                   q_ref, k_ref, v_ref, u_ref, vln_ref, pz_ref):
    xb = x_ref[...].astype(bf16)
    cos = cos_ref[...]
    sa = sa_ref[...]
    sb = sb_ref[...]
    for base, out_ref, rope, scale in ((0, q_ref, True, HEAD_DIM ** -0.5),
                                       (ATTN_WIDTH, k_ref, True, 1.0),
                                       (2 * ATTN_WIDTH, v_ref, False, 1.0)):
        y = jnp.dot(xb, w_ref[:, base:base + ATTN_WIDTH], preferred_element_type=f32)
        for p in range(ATTN_WIDTH // LANES):
            yp = y[:, p * LANES:(p + 1) * LANES]
            if rope:
                yp = yp * cos + pltpu.roll(yp, LANES - ROPE_DIM // 2, 1) * sa + pltpu.roll(yp, ROPE_DIM // 2, 1) * sb
            if scale != 1.0:
                yp = yp * scale
            out_ref[p] = yp.astype(bf16)
    o3 = 3 * ATTN_WIDTH
    y = jnp.dot(xb, w_ref[:, o3:o3 + 3 * GMLP_WIDTH], preferred_element_type=f32)
    u_ref[...] = jax.nn.gelu(y[:, :GMLP_WIDTH])
    gv = jax.nn.gelu(y[:, GMLP_WIDTH:2 * GMLP_WIDTH])
    mu = _group_mean(gv)
    d = gv - mu
    var = _group_mean(d * d)
    vln_ref[...] = (d * lax.rsqrt(var + LN_EPS) * lng_ref[...] + lnb_ref[...]).astype(bf16)
    pz_ref[...] = y[:, 2 * GMLP_WIDTH:]


def _inproj(x, w_in, cos_t, sa_t, sb_t, lng, lnb, seq):
    t = x.shape[0]
    nt = t // ROW_TILE
    tiles_per_seq = seq // ROW_TILE
    npair = ATTN_WIDTH // LANES
    tab_spec = pl.BlockSpec((ROW_TILE, LANES), lambda i: (i % tiles_per_seq, 0))
    row = lambda w: pl.BlockSpec((ROW_TILE, w), lambda i: (i, 0))
    const = lambda shape: pl.BlockSpec(shape, lambda i: (0,) * len(shape))
    pair_spec = pl.BlockSpec((npair, ROW_TILE, LANES), lambda i: (0, i, 0))
    pair_shape = jax.ShapeDtypeStruct((npair, t, LANES), bf16)
    return pl.pallas_call(
        _inproj_kernel,
        grid=(nt,),
        in_specs=[row(D_MODEL), const((D_MODEL, IN_COLS)), tab_spec, tab_spec, tab_spec,
                  const((1, GMLP_WIDTH)), const((1, GMLP_WIDTH))],
        out_specs=[pair_spec, pair_spec, pair_spec, row(GMLP_WIDTH), row(GMLP_WIDTH), row(POOL_WIDTH)],
        out_shape=[pair_shape, pair_shape, pair_shape,
                   jax.ShapeDtypeStruct((t, GMLP_WIDTH), f32),
                   jax.ShapeDtypeStruct((t, GMLP_WIDTH), bf16),
                   jax.ShapeDtypeStruct((t, POOL_WIDTH), f32)],
        compiler_params=pltpu.CompilerParams(dimension_semantics=("parallel",), vmem_limit_bytes=VMEM_LIMIT),
        name="inproj",
    )(x, w_in, cos_t, sa_t, sb_t, lng, lnb)


def _attn_kernel(q_ref, k_ref, v_ref, o_ref, qs, ks, vs, o0, o1, o2, l0s, l1s, l2s, *, seq):
    os_ = (o0, o1, o2)
    ls_ = (l0s, l1s, l2s)
    qs[...] = q_ref[...].astype(f32)
    ks[...] = k_ref[...].astype(f32)
    vs[...] = v_ref[...].astype(f32)
    lane = lax.broadcasted_iota(jnp.int32, (ATTN_QBLK, LANES), 1)
    head_a = lane < HEAD_DIM

    for ci, (_, dil) in enumerate(DILATED_CONFIGS):
        length = seq // dil
        nblk = length // ATTN_QBLK
        win = min(2 * ATTN_QBLK, length)
        shift = dil.bit_length() - 1
        col_minus_row = (lax.broadcasted_iota(jnp.int32, (ATTN_QBLK, win), 1)
                         - lax.broadcasted_iota(jnp.int32, (ATTN_QBLK, win), 0))

        def rows(start, size, dil=dil):
            if dil == 1:
                return pl.ds(pl.multiple_of(start, 8), size)
            return pl.ds(start, size, stride=dil)

        def body(u, carry, ci=ci, dil=dil, length=length, win=win, shift=shift,
                 col_minus_row=col_minus_row, rows=rows):
            c = u & (dil - 1)
            i = u >> shift
            k0 = jnp.clip(i * ATTN_QBLK - ATTN_HALF, 0, length - win)
            delta = k0 - i * ATTN_QBLK
            qrows = rows(c + dil * ATTN_QBLK * i, ATTN_QBLK)
            krows = rows(c + dil * k0, win)
            q = qs[qrows, :]
            kb = ks[krows, :].astype(bf16)
            vb = vs[krows, :].astype(bf16)
            mask = jnp.abs(col_minus_row + delta) <= ATTN_HALF
            res = []
            for hm in (head_a, jnp.logical_not(head_a)):
                qh = jnp.where(hm, q, 0.0).astype(bf16)
                s = lax.dot_general(qh, kb, (((1,), (1,)), ((), ())), preferred_element_type=f32)
                s = jnp.where(mask, s, NEG_INF)
                m = jnp.max(s, axis=-1, keepdims=True)
                p = jnp.exp(s - m)
                l = jnp.sum(p, axis=-1, keepdims=True)
                o = jnp.dot(p.astype(bf16), vb, preferred_element_type=f32) / l
                res.append((o, m + jnp.log(l)))
            os_[ci][qrows, :] = jnp.where(head_a, res[0][0], res[1][0])
            ls_[ci][qrows, :] = jnp.where(head_a, res[0][1], res[1][1])
            return carry

        lax.fori_loop(0, dil * nblk, body, 0)

    chunk = 256

    def combine(j, carry):
        r = pl.ds(pl.multiple_of(j * chunk, chunk), chunk)
        l0, l1, l2 = l0s[r, :], l1s[r, :], l2s[r, :]
        mx = jnp.maximum(jnp.maximum(l0, l1), l2)
        w0, w1, w2 = jnp.exp(l0 - mx), jnp.exp(l1 - mx), jnp.exp(l2 - mx)
        num = w0 * o0[r, :] + w1 * o1[r, :] + w2 * o2[r, :]
        o_ref[r, :] = (num / (w0 + w1 + w2)).astype(bf16)
        return carry

    lax.fori_loop(0, seq // chunk, combine, 0)


def _attention(qp, kp, vp, batch, seq):
    npair = qp.shape[0]
    t = batch * seq
    in_spec = pl.BlockSpec((None, seq, LANES), lambda p, b: (p, b, 0))
    return pl.pallas_call(
        functools.partial(_attn_kernel, seq=seq),
        grid=(npair, batch),
        in_specs=[in_spec, in_spec, in_spec],
        out_specs=pl.BlockSpec((seq, LANES), lambda p, b: (b, p)),
        out_shape=jax.ShapeDtypeStruct((t, ATTN_WIDTH), bf16),
        scratch_shapes=[pltpu.VMEM((seq, LANES), f32)] * (3 + 2 * len(DILATED_CONFIGS)),
        compiler_params=pltpu.CompilerParams(dimension_semantics=("parallel", "parallel"),
                                             vmem_limit_bytes=VMEM_LIMIT),
        name="dilated_attn",
    )(qp, kp, vp)


def _mixer_kernel(u_ref, vln_ref, pz_ref, prev_ref, next_ref, ws_ref, bs_ref, wp_ref, sc_ref,
                  gm_ref, pool_ref, z0, z1, z2, z3, z4, *, seq):
    tm = ROW_TILE
    h = POOL_HALO
    lane = lax.broadcasted_iota(jnp.int32, (GMLP_CHUNK, GMLP_WIDTH), 1)
    for cc in range(tm // GMLP_CHUNK):
        r = slice(cc * GMLP_CHUNK, (cc + 1) * GMLP_CHUNK)
        vc = vln_ref[r, :]
        sv = jnp.zeros((GMLP_CHUNK, GMLP_WIDTH), f32)
        for g in range(GMLP_GROUPS):
            svg = jnp.dot(ws_ref[g], vc, preferred_element_type=f32)
            sv = jnp.where((lane >= g * HEAD_DIM) & (lane < (g + 1) * HEAD_DIM), svg, sv)
        gm_ref[r, :] = (u_ref[r, :] * (sv + bs_ref[...])).astype(bf16)

    i = pl.program_id(0)
    tiles_per_seq = seq // tm
    pos0 = (i % tiles_per_seq) * tm
    first = pos0 == 0
    last = pos0 + tm == seq
    z0[0:h, :] = jnp.where(first, 0.0, prev_ref[...])
    z0[h:h + tm, :] = pz_ref[...]
    z0[h + tm:h + tm + h, :] = jnp.where(last, 0.0, next_ref[...])
    z1[8:tm + 2 * h - 8, :] = z0[8:tm + 2 * h - 8, :] + z0[7:tm + 2 * h - 9, :]
    z2[16:tm + 2 * h - 16, :] = z1[15:tm + 2 * h - 17, :] + z1[17:tm + 2 * h - 15, :]
    z3[24:tm + 2 * h - 24, :] = z2[22:tm + 2 * h - 26, :] + z2[26:tm + 2 * h - 22, :]
    z4[h:h + tm, :] = z3[h - 4:h + tm - 4, :] + z3[h + 4:h + tm + 4, :]
    lane_t = lax.broadcasted_iota(jnp.int32, (tm, POOL_WIDTH), 1)
    pos = pos0 + lax.broadcasted_iota(jnp.int32, (tm, POOL_WIDTH), 0)
    zc = z0[h:h + tm, :]
    pooled = jnp.zeros((tm, POOL_WIDTH), f32)
    for g, (w, zw) in enumerate(zip(POOL_WINDOWS, (z1, z2, z3, z4))):
        left = w // 2
        right = w - 1 - left
        lo = jnp.maximum(pos - left, 0)
        hi = jnp.minimum(pos + right + 1, seq)
        cnt = (hi - lo).astype(f32)
        val = zw[h:h + tm, :] / cnt - zc
        pooled = jnp.where((lane_t >= g * HEAD_DIM) & (lane_t < (g + 1) * HEAD_DIM), val, pooled)
    y = jnp.dot(pooled.astype(bf16), wp_ref[...], preferred_element_type=f32)
    pool_ref[...] = (y * sc_ref[...]).astype(bf16)


def _mixer(u, vln, pz, ws, bs_full, wp_bd, scale, seq):
    t = u.shape[0]
    tm, h = ROW_TILE, POOL_HALO
    nt = t // tm
    hb = tm // h
    row = lambda w: pl.BlockSpec((tm, w), lambda i: (i, 0))
    const = lambda shape: pl.BlockSpec(shape, lambda i: (0,) * len(shape))
    prev_spec = pl.BlockSpec((h, POOL_WIDTH), lambda i: (jnp.maximum(i * hb - 1, 0), 0))
    next_spec = pl.BlockSpec((h, POOL_WIDTH), lambda i: (jnp.minimum((i + 1) * hb, t // h - 1), 0))
    zbuf = pltpu.VMEM((tm + 2 * h, POOL_WIDTH), f32)
    return pl.pallas_call(
        functools.partial(_mixer_kernel, seq=seq),
        grid=(nt,),
        in_specs=[row(GMLP_WIDTH), row(GMLP_WIDTH), row(POOL_WIDTH), prev_spec, next_spec,
                  const((GMLP_GROUPS, GMLP_CHUNK, GMLP_CHUNK)), const((GMLP_CHUNK, GMLP_WIDTH)),
                  const((POOL_WIDTH, POOL_WIDTH)), const((1, POOL_WIDTH))],
        out_specs=[row(GMLP_WIDTH), row(POOL_WIDTH)],
        out_shape=[jax.ShapeDtypeStruct((t, GMLP_WIDTH), bf16), jax.ShapeDtypeStruct((t, POOL_WIDTH), bf16)],
        scratch_shapes=[zbuf] * 5,
        compiler_params=pltpu.CompilerParams(dimension_semantics=("parallel",), vmem_limit_bytes=VMEM_LIMIT),
        name="mixer",
    )(u, vln, pz, pz, pz, ws, bs_full, wp_bd, scale)


def _top2_of4(b, s):
    v1, i1, s1 = b[0], jnp.zeros(b[0].shape, jnp.int32), s[0]
    for j in range(1, 4):
        gt = b[j] > v1
        v1 = jnp.where(gt, b[j], v1)
        i1 = jnp.where(gt, j, i1)
        s1 = jnp.where(gt, s[j], s1)
    v2 = jnp.full(b[0].shape, -jnp.inf, f32)
    i2 = jnp.full(b[0].shape, -1, jnp.int32)
    s2 = jnp.zeros(b[0].shape, f32)
    for j in range(4):
        cand = jnp.where(i1 != j, b[j], -jnp.inf) > v2
        v2 = jnp.where(cand, b[j], v2)
        i2 = jnp.where(cand, j, i2)
        s2 = jnp.where(cand, s[j], s2)
    return v1 + v2, i1, i2, s1, s2


def _outproj_kernel(attn_ref, gm_ref, pool_ref, x_ref, w_ref, g_ref, b_ref, rw_ref, rb_ref, tri_ref,
                    xe_ref, cnt_ref, carry):
    tm = ROW_TILE

    @pl.when(pl.program_id(0) == 0)
    def _():
        carry[...] = jnp.zeros_like(carry)

    o1, o2 = ATTN_WIDTH, ATTN_WIDTH + GMLP_WIDTH
    hmix = jnp.dot(attn_ref[...], w_ref[0:o1, :], preferred_element_type=f32)
    hmix = hmix + jnp.dot(gm_ref[...], w_ref[o1:o2, :], preferred_element_type=f32)
    hmix = hmix + jnp.dot(pool_ref[...], w_ref[o2:, :], preferred_element_type=f32)
    x1 = _layernorm(DEEPNORM_ALPHA * x_ref[...] + hmix, g_ref[...], b_ref[...])
    xe_ref[:, 0:D_MODEL] = x1

    logits = lax.dot_general(rw_ref[...], x1, (((1,), (1,)), ((), ())),
                             precision=lax.Precision.HIGHEST, preferred_element_type=f32)
    scores = 1.0 / (1.0 + jnp.exp(-logits))
    biased = scores + rb_ref[...]
    brow = [biased[e:e + 1, :] for e in range(N_EXPERTS)]
    srow = [scores[e:e + 1, :] for e in range(N_EXPERTS)]
    best = None
    for g in range(N_EXPERT_GROUPS):
        sl = slice(g * EXPERTS_PER_GROUP, (g + 1) * EXPERTS_PER_GROUP)
        gs, i1, i2, s1, s2 = _top2_of4(brow[sl], srow[sl])
        if best is None:
            best = (gs, jnp.zeros(gs.shape, jnp.int32), i1, i2, s1, s2)
        else:
            gt = gs > best[0]
            best = (jnp.where(gt, gs, best[0]), jnp.where(gt, g, best[1]), jnp.where(gt, i1, best[2]),
                    jnp.where(gt, i2, best[3]), jnp.where(gt, s1, best[4]), jnp.where(gt, s2, best[5]))
    _, gsel, i1, i2, s1, s2 = best
    den = s1 + s2
    ga, gb = s1 / den, s2 / den
    first_lo = i1 < i2
    lo = jnp.minimum(i1, i2)
    hi = jnp.maximum(i1, i2)
    g_lo = jnp.where(first_lo, ga, gb)
    g_hi = jnp.where(first_lo, gb, ga)
    pidx = jnp.where(lo == 0, hi - 1, jnp.where(lo == 1, hi + 1, N_PAIRS - 1))
    cls = gsel * N_PAIRS + pidx

    onehot = lax.broadcasted_iota(jnp.int32, (CLASS_ROWS, tm), 0) == cls
    ohf = jnp.where(onehot, 1.0, 0.0)
    prefix = jnp.dot(ohf.astype(bf16), tri_ref[...], preferred_element_type=f32)
    base = carry[:, 0:1]
    rank = jnp.sum(ohf * (prefix + base), axis=0, keepdims=True)
    new_carry = carry[...] + jnp.sum(ohf, axis=1, keepdims=True)
    carry[...] = new_carry
    cnt_ref[...] = new_carry

    srow_id = lax.broadcasted_iota(jnp.int32, (8, tm), 0)
    side8 = jnp.where(srow_id == 0, g_lo, jnp.where(srow_id == 1, g_hi, jnp.where(
        srow_id == 2, cls.astype(f32), jnp.where(srow_id == 3, rank, 0.0))))
    side = jnp.concatenate([side8, jnp.zeros((SIDE - 8, tm), f32)], axis=0)
    xe_ref[:, D_MODEL:] = side.T


def _outproj(attn, gm, pool, x, w_out, g, b, rw_t, rb, tri):
    t = x.shape[0]
    tm = ROW_TILE
    row = lambda w: pl.BlockSpec((tm, w), lambda i: (i, 0))
    const = lambda shape: pl.BlockSpec(shape, lambda i: (0,) * len(shape))
    return pl.pallas_call(
        _outproj_kernel,
        grid=(t // tm,),
        in_specs=[row(ATTN_WIDTH), row(GMLP_WIDTH), row(POOL_WIDTH), row(D_MODEL), const((D_MODEL, D_MODEL)),
                  const((1, D_MODEL)), const((1, D_MODEL)), const((N_EXPERTS, D_MODEL)), const((N_EXPERTS, 1)),
                  const((tm, tm))],
        out_specs=[row(D_MODEL + SIDE), const((CLASS_ROWS, LANES))],
        out_shape=[jax.ShapeDtypeStruct((t, D_MODEL + SIDE), f32), jax.ShapeDtypeStruct((CLASS_ROWS, LANES), f32)],
        scratch_shapes=[pltpu.VMEM((CLASS_ROWS, LANES), f32)],
        compiler_params=pltpu.CompilerParams(dimension_semantics=("arbitrary",), vmem_limit_bytes=VMEM_LIMIT),
        name="outproj_ln_router",
    )(attn, gm, pool, x, w_out, g, b, rw_t, rb, tri)


def _permute_kernel(idx_ref, zflag_ref, src_ref, *rest, scatter, nzero, width):
    if scatter:
        dst_ref, zeros, sem, zsem = rest
    else:
        dst_ref, sem = rest
    step = pl.program_id(0)

    if scatter:
        def zcopy(j):
            return pltpu.make_async_copy(zeros, dst_ref.at[pl.ds(pl.multiple_of(j * MOE_BLK, MOE_BLK), MOE_BLK), :],
                                         zsem)

        @pl.when(step == 0)
        def _():
            zeros[...] = jnp.zeros_like(zeros)

            def zstart(j, c):
                @pl.when(zflag_ref[j] != 0)
                def _():
                    zcopy(j).start()
                return c

            def zwait(j, c):
                @pl.when(zflag_ref[j] != 0)
                def _():
                    zcopy(j).wait()
                return c

            lax.fori_loop(0, nzero, zstart, 0)
            lax.fori_loop(0, nzero, zwait, 0)

    base = step * PERM_CHUNK

    def row_copy(r):
        other = idx_ref[0, 0, r]
        if scatter:
            return pltpu.make_async_copy(src_ref.at[pl.ds(base + r, 1), :], dst_ref.at[pl.ds(other, 1), :], sem)
        return pltpu.make_async_copy(src_ref.at[pl.ds(other, 1), :], dst_ref.at[pl.ds(base + r, 1), :], sem)

    def start(r, c):
        row_copy(r).start()
        return c

    def wait(r, c):
        row_copy(r).wait()
        return c

    lax.fori_loop(0, PERM_CHUNK, start, 0)
    lax.fori_loop(0, PERM_CHUNK, wait, 0)


def _permute(idx, zflag, src, n_dst, width, scatter):
    t = idx.shape[0]
    nsteps = t // PERM_CHUNK
    idx3 = idx.reshape(nsteps, 1, PERM_CHUNK)
    nzero = zflag.shape[0]
    scratch = [pltpu.SemaphoreType.DMA(())]
    if scatter:
        scratch = [pltpu.VMEM((MOE_BLK, width), f32), pltpu.SemaphoreType.DMA(()), pltpu.SemaphoreType.DMA(())]
    return pl.pallas_call(
        functools.partial(_permute_kernel, scatter=scatter, nzero=nzero, width=width),
        grid=(nsteps,),
        in_specs=[pl.BlockSpec((1, 1, PERM_CHUNK), lambda i: (i, 0, 0), memory_space=pltpu.SMEM),
                  pl.BlockSpec(memory_space=pltpu.SMEM),
                  pl.BlockSpec(memory_space=pl.ANY)],
        out_specs=pl.BlockSpec(memory_space=pl.ANY),
        out_shape=jax.ShapeDtypeStruct((n_dst, width), f32),
        scratch_shapes=scratch,
        compiler_params=pltpu.CompilerParams(dimension_semantics=("arbitrary",)),
        name="dispatch_rows" if scatter else "undispatch_rows",
    )(idx3, zflag, src)


def _expert_kernel(elo_ref, ehi_ref, nused_ref, xb_ref, wg_lo, wu_lo, wd_lo, wg_hi, wu_hi, wd_hi, g_ref, b_ref,
                   out_ref):
    j = pl.program_id(0)

    @pl.when(j < nused_ref[0])
    def _():
        x = xb_ref[:, 0:D_MODEL]
        gates = xb_ref[:, D_MODEL:D_MODEL + SIDE]
        xb = x.astype(bf16)
        moe = jnp.zeros_like(x)
        for col, (wg, wu, wd) in enumerate(((wg_lo, wu_lo, wd_lo), (wg_hi, wu_hi, wd_hi))):
            a = jnp.dot(xb, wg[...], preferred_element_type=f32)
            up = jnp.dot(xb, wu[...], preferred_element_type=f32)
            hidden = (a / (1.0 + jnp.exp(-a))) * up
            y = jnp.dot(hidden.astype(bf16), wd[...], preferred_element_type=f32)
            moe = moe + y * gates[:, col:col + 1]
        out_ref[...] = _layernorm(DEEPNORM_ALPHA * x + moe, g_ref[...], b_ref[...])

    @pl.when(j >= nused_ref[0])
    def _():
        out_ref[...] = jnp.zeros_like(out_ref)


def _experts(elo, ehi, nused, buf, wg, wu, wd, g, b):
    p = buf.shape[0]
    nb = p // MOE_BLK
    wspec = lambda shape, which: pl.BlockSpec(
        (None,) + shape, (lambda j, elo, ehi, nu: (elo[j], 0, 0)) if which == 0 else (lambda j, elo, ehi, nu: (ehi[j], 0, 0)))
    up_shape, down_shape = (D_MODEL, EXPERT_FF), (EXPERT_FF, D_MODEL)
    const = lambda shape: pl.BlockSpec(shape, lambda j, elo, ehi, nu: (0,) * len(shape))
    grid_spec = pltpu.PrefetchScalarGridSpec(
        num_scalar_prefetch=3,
        grid=(nb,),
        in_specs=[pl.BlockSpec((MOE_BLK, D_MODEL + SIDE), lambda j, elo, ehi, nu: (j, 0)),
                  wspec(up_shape, 0), wspec(up_shape, 0), wspec(down_shape, 0),
                  wspec(up_shape, 1), wspec(up_shape, 1), wspec(down_shape, 1),
                  const((1, D_MODEL)), const((1, D_MODEL))],
        out_specs=pl.BlockSpec((MOE_BLK, D_MODEL), lambda j, elo, ehi, nu: (j, 0)),
    )
    return pl.pallas_call(
        _expert_kernel,
        grid_spec=grid_spec,
        out_shape=jax.ShapeDtypeStruct((p, D_MODEL), f32),
        compiler_params=pltpu.CompilerParams(dimension_semantics=("arbitrary",), vmem_limit_bytes=VMEM_LIMIT),
        name="expert_pairs",
    )(elo, ehi, nused, buf, wg, wu, wd, wg, wu, wd, g, b)


def _rope_tables(seq):
    half = ROPE_DIM // 2
    inv_freq = jnp.power(jnp.float32(ROPE_THETA), -jnp.arange(half, dtype=f32) / half)
    ang = jnp.arange(seq, dtype=f32)[:, None] * inv_freq[None, :]
    lane = np.arange(LANES) % HEAD_DIM
    fidx = lane % half
    in_rope = lane < ROPE_DIM
    first = lane < half
    cos_l = jnp.cos(ang)[:, fidx]
    sin_l = jnp.sin(ang)[:, fidx]
    cos_t = jnp.where(in_rope[None, :], cos_l, 1.0)
    sa_t = jnp.where((in_rope & first)[None, :], -sin_l, 0.0)
    sb_t = jnp.where((in_rope & ~first)[None, :], sin_l, 0.0)
    return cos_t, sa_t, sb_t


_PAIRS = [(a, b) for a in range(EXPERTS_PER_GROUP) for b in range(a + 1, EXPERTS_PER_GROUP)]


def _dispatch_plan(side, counts):
    t = side.shape[0]
    cls = side[:, 2].astype(jnp.int32)
    rank = side[:, 3].astype(jnp.int32)
    counts = counts.astype(jnp.int32)
    padded = ((counts + MOE_BLK - 1) // MOE_BLK) * MOE_BLK
    pend = jnp.cumsum(padded)
    pstart = pend - padded
    dest = pstart[cls] + rank
    nb = (t + N_CLASSES * MOE_BLK) // MOE_BLK
    blk_start = jnp.arange(nb, dtype=jnp.int32) * MOE_BLK
    blk_cls = jnp.minimum(jnp.searchsorted(pend, blk_start, side="right"), N_CLASSES - 1).astype(jnp.int32)
    valid_end = pstart[blk_cls] + counts[blk_cls]
    zflag = ((blk_start + MOE_BLK > valid_end) | (blk_start >= pend[-1])).astype(jnp.int32)
    pair_lo = jnp.asarray([p[0] for p in _PAIRS], jnp.int32)
    pair_hi = jnp.asarray([p[1] for p in _PAIRS], jnp.int32)
    grp = blk_cls // N_PAIRS
    elo = grp * EXPERTS_PER_GROUP + pair_lo[blk_cls % N_PAIRS]
    ehi = grp * EXPERTS_PER_GROUP + pair_hi[blk_cls % N_PAIRS]
    nused = (pend[-1] // MOE_BLK).astype(jnp.int32).reshape(1)
    return dest, zflag, elo, ehi, nused


def kernel(x, w_in, w_out, gmlp_ln_g, gmlp_ln_b, gmlp_w_s, gmlp_b_s, pool_w, pool_scale, ln1_g, ln1_b,
           router_w, router_bias, w_gate, w_up, w_down, ln2_g, ln2_b):
    batch, seq, d = x.shape
    t = batch * seq
    assert d == D_MODEL and seq % ROW_TILE == 0 and t % PERM_CHUNK == 0
    cos_t, sa_t, sb_t = _rope_tables(seq)
    tri = (np.arange(ROW_TILE)[:, None] < np.arange(ROW_TILE)[None, :])
    tri = jnp.asarray(tri, bf16)
    rw_t = router_w.T
    rb = router_bias.reshape(N_EXPERTS, 1)
    n_rows = t + N_CLASSES * MOE_BLK
    dummy_flag = jnp.zeros((1,), jnp.int32)

    xf = x.reshape(t, d)
    for l in range(DEPTH):
        lng = gmlp_ln_g[l].reshape(1, GMLP_WIDTH)
        lnb = gmlp_ln_b[l].reshape(1, GMLP_WIDTH)
        qp, kp, vp, u, vln, pz = _inproj(xf, w_in[l].astype(bf16), cos_t, sa_t, sb_t, lng, lnb, seq)
        attn = _attention(qp, kp, vp, batch, seq)
        bs_full = jnp.repeat(gmlp_b_s[l].T, HEAD_DIM, axis=1)
        wp_bd = jnp.zeros((POOL_WIDTH, POOL_WIDTH), f32)
        for g in range(len(POOL_WINDOWS)):
            wp_bd = wp_bd.at[g * HEAD_DIM:(g + 1) * HEAD_DIM, g * HEAD_DIM:(g + 1) * HEAD_DIM].set(pool_w[l, g])
        wp_bd = wp_bd.astype(bf16)
        gm, pool = _mixer(u, vln, pz, gmlp_w_s[l].astype(bf16), bs_full, wp_bd,
                          pool_scale[l].reshape(1, POOL_WIDTH), seq)
        xe, counts = _outproj(attn, gm, pool, xf, w_out[l].astype(bf16), ln1_g[l].reshape(1, d),
                              ln1_b[l].reshape(1, d), rw_t, rb, tri)
        dest, zflag, elo, ehi, nused = _dispatch_plan(xe[:, D_MODEL:D_MODEL + 4], counts[:N_CLASSES, 0])
        buf = _permute(dest, zflag, xe, n_rows, D_MODEL + SIDE, scatter=True)
        ys = _experts(elo, ehi, nused, buf, w_gate[l].astype(bf16), w_up[l].astype(bf16),
                      w_down[l].astype(bf16), ln2_g[l].reshape(1, d), ln2_b[l].reshape(1, d))
        xf = _permute(dest, dummy_flag, ys, t, D_MODEL, scatter=False)
    return xf.reshape(batch, seq, d)
```

```python
import functools

import jax
import jax.numpy as jnp
import numpy as np
from jax import lax
from jax.experimental import pallas as pl
from jax.experimental.pallas import tpu as pltpu

f32 = jnp.float32
bf16 = jnp.bfloat16

D_MODEL = 1024
HEAD_DIM = 64
ATTN_WIDTH = 512
DILATED_CONFIGS = ((128, 1), (512, 4), (2048, 16))
ATTN_HALF = 64
ROPE_THETA = 500000.0
ROPE_DIM = 16
GMLP_WIDTH = 256
GMLP_GROUPS = 4
GMLP_CHUNK = 128
POOL_WIDTH = 256
POOL_WINDOWS = (2, 4, 8, 16)
IN_COLS = 3 * ATTN_WIDTH + 2 * GMLP_WIDTH + POOL_WIDTH
N_EXPERTS = 16
N_EXPERT_GROUPS = 4
EXPERTS_PER_GROUP = 4
EXPERT_FF = 512
DEPTH = 2
DEEPNORM_ALPHA = float((2 * DEPTH) ** 0.25)
LN_EPS = 1e-5
NEG_INF = -1e30

LANES = 128
ROW_TILE = 512
ATTN_QBLK = 128
POOL_HALO = 32
N_PAIRS = 6
N_CLASSES = N_EXPERT_GROUPS * N_PAIRS
CLASS_ROWS = 32
MOE_BLK = 256
SIDE = LANES
PERM_CHUNK = 512
VMEM_LIMIT = 48 * 1024 * 1024


def _layernorm(y, g, b):
    mu = jnp.mean(y, axis=-1, keepdims=True)
    d = y - mu
    var = jnp.mean(d * d, axis=-1, keepdims=True)
    return d * lax.rsqrt(var + LN_EPS) * g + b


def _group_mean(v):
    lane = lax.broadcasted_iota(jnp.int32, v.shape, 1)
    out = jnp.zeros_like(v)
    for g in range(GMLP_GROUPS):
        m = (lane >= g * HEAD_DIM) & (lane < (g + 1) * HEAD_DIM)
        s = jnp.sum(jnp.where(m, v, 0.0), axis=-1, keepdims=True)
        out = jnp.where(m, s, out)
    return out * (1.0 / HEAD_DIM)


def _inproj_kernel(x_ref, w_ref, cos_ref, sa_ref, sb_ref, lng_ref, lnb_ref,
                   q_ref, k_ref, v_ref, u_ref, vln_ref, pz_ref):
    xb = x_ref[...].astype(bf16)
    cos = cos_ref[...]
    sa = sa_ref[...]
    sb = sb_ref[...]
    for base, out_ref, rope, scale in ((0, q_ref, True, HEAD_DIM ** -0.5),
                                       (ATTN_WIDTH, k_ref, True, 1.0),
                                       (2 * ATTN_WIDTH, v_ref, False, 1.0)):
        y = jnp.dot(xb, w_ref[:, base:base + ATTN_WIDTH], preferred_element_type=f32)
        for p in range(ATTN_WIDTH // LANES):
            yp = y[:, p * LANES:(p + 1) * LANES]
            if rope:
                yp = yp * cos + pltpu.roll(yp, LANES - ROPE_DIM // 2, 1) * sa + pltpu.roll(yp, ROPE_DIM // 2, 1) * sb
            if scale != 1.0:
                yp = yp * scale
            out_ref[p] = yp.astype(bf16)
    o3 = 3 * ATTN_WIDTH
    y = jnp.dot(xb, w_ref[:, o3:o3 + 3 * GMLP_WIDTH], preferred_element_type=f32)
    u_ref[...] = jax.nn.gelu(y[:, :GMLP_WIDTH])
    gv = jax.nn.gelu(y[:, GMLP_WIDTH:2 * GMLP_WIDTH])
    mu = _group_mean(gv)
    d = gv - mu
    var = _group_mean(d * d)
    vln_ref[...] = (d * lax.rsqrt(var + LN_EPS) * lng_ref[...] + lnb_ref[...]).astype(bf16)
    pz_ref[...] = y[:, 2 * GMLP_WIDTH:]


def _inproj(x, w_in, cos_t, sa_t, sb_t, lng, lnb, seq):
    t = x.shape[0]
    nt = t // ROW_TILE
    tiles_per_seq = seq // ROW_TILE
    npair = ATTN_WIDTH // LANES
    tab_spec = pl.BlockSpec((ROW_TILE, LANES), lambda i: (i % tiles_per_seq, 0))
    row = lambda w: pl.BlockSpec((ROW_TILE, w), lambda i: (i, 0))
    const = lambda shape: pl.BlockSpec(shape, lambda i: (0,) * len(shape))
    pair_spec = pl.BlockSpec((npair, ROW_TILE, LANES), lambda i: (0, i, 0))
    pair_shape = jax.ShapeDtypeStruct((npair, t, LANES), bf16)
    return pl.pallas_call(
        _inproj_kernel,
        grid=(nt,),
        in_specs=[row(D_MODEL), const((D_MODEL, IN_COLS)), tab_spec, tab_spec, tab_spec,
                  const((1, GMLP_WIDTH)), const((1, GMLP_WIDTH))],
        out_specs=[pair_spec, pair_spec, pair_spec, row(GMLP_WIDTH), row(GMLP_WIDTH), row(POOL_WIDTH)],
        out_shape=[pair_shape, pair_shape, pair_shape,
                   jax.ShapeDtypeStruct((t, GMLP_WIDTH), f32),
                   jax.ShapeDtypeStruct((t, GMLP_WIDTH), bf16),
                   jax.ShapeDtypeStruct((t, POOL_WIDTH), f32)],
        compiler_params=pltpu.CompilerParams(dimension_semantics=("parallel",), vmem_limit_bytes=VMEM_LIMIT),
        name="inproj",
    )(x, w_in, cos_t, sa_t, sb_t, lng, lnb)


def _attn_kernel(q_ref, k_ref, v_ref, bias_ref, o_ref, qs, ks, vs, o0, o1, o2, l0s, l1s, l2s, *, seq):
    os_ = (o0, o1, o2)
    ls_ = (l0s, l1s, l2s)
    qs[...] = q_ref[...].astype(f32)
    ks[...] = k_ref[...].astype(f32)
    vs[...] = v_ref[...].astype(f32)
    lane = lax.broadcasted_iota(jnp.int32, (ATTN_QBLK, LANES), 1)
    head_a = lane < HEAD_DIM

    for ci, (_, dil) in enumerate(DILATED_CONFIGS):
        length = seq // dil
        nblk = length // ATTN_QBLK
        win = min(2 * ATTN_QBLK, length)
        shift = dil.bit_length() - 1

        def rows(start, size, dil=dil):
            if dil == 1:
                return pl.ds(pl.multiple_of(start, 8), size)
            return pl.ds(start, size, stride=dil)

        def body(u, carry, ci=ci, dil=dil, length=length, win=win, shift=shift, rows=rows):
            c = u & (dil - 1)
            i = u >> shift
            k0 = jnp.clip(i * ATTN_QBLK - ATTN_HALF, 0, length - win)
            delta = k0 - i * ATTN_QBLK
            qrows = rows(c + dil * ATTN_QBLK * i, ATTN_QBLK)
            krows = rows(c + dil * k0, win)
            q = qs[qrows, :]
            kb = ks[krows, :].astype(bf16)
            vb = vs[krows, :].astype(bf16)
            bias = bias_ref[lax.shift_right_logical(-delta, ATTN_HALF.bit_length() - 1), :, 0:win]
            q2 = jnp.concatenate([jnp.where(head_a, q, 0.0), jnp.where(head_a, 0.0, q)], axis=0).astype(bf16)
            s = lax.dot_general(q2, kb, (((1,), (1,)), ((), ())), preferred_element_type=f32) + bias
            m = jnp.max(s, axis=-1, keepdims=True)
            p = jnp.exp(s - m).astype(bf16)
            ol = jnp.dot(p, jnp.concatenate([vb, jnp.ones_like(vb)], axis=1), preferred_element_type=f32)
            l = ol[:, LANES:]
            o = ol[:, :LANES] / l
            lse = m + jnp.log(l)
            os_[ci][qrows, :] = jnp.where(head_a, o[:ATTN_QBLK], o[ATTN_QBLK:])
            ls_[ci][qrows, :] = jnp.where(head_a, lse[:ATTN_QBLK], lse[ATTN_QBLK:])
            return carry

        lax.fori_loop(0, dil * nblk, body, 0, unroll=8)

    chunk = 256

    def combine(j, carry):
        r = pl.ds(pl.multiple_of(j * chunk, chunk), chunk)
        l0, l1, l2 = l0s[r, :], l1s[r, :], l2s[r, :]
        mx = jnp.maximum(jnp.maximum(l0, l1), l2)
        w0, w1, w2 = jnp.exp(l0 - mx), jnp.exp(l1 - mx), jnp.exp(l2 - mx)
        num = w0 * o0[r, :] + w1 * o1[r, :] + w2 * o2[r, :]
        o_ref[r, :] = (num / (w0 + w1 + w2)).astype(bf16)
        return carry

    lax.fori_loop(0, seq // chunk, combine, 0)


def _attention(qp, kp, vp, batch, seq):
    npair = qp.shape[0]
    t = batch * seq
    in_spec = pl.BlockSpec((None, seq, LANES), lambda p, b: (p, b, 0))
    win = 2 * ATTN_QBLK
    rr = np.arange(2 * ATTN_QBLK)[None, :, None] % ATTN_QBLK
    cc = np.arange(win)[None, None, :]
    dd = np.arange(3)[:, None, None] * ATTN_HALF
    bias = jnp.asarray(np.where(np.abs(cc - dd - rr) <= ATTN_HALF, 0.0, NEG_INF), f32)
    return pl.pallas_call(
        functools.partial(_attn_kernel, seq=seq),
        grid=(npair, batch),
        in_specs=[in_spec, in_spec, in_spec, pl.BlockSpec(bias.shape, lambda p, b: (0, 0, 0))],
        out_specs=pl.BlockSpec((seq, LANES), lambda p, b: (b, p)),
        out_shape=jax.ShapeDtypeStruct((t, ATTN_WIDTH), bf16),
        scratch_shapes=[pltpu.VMEM((seq, LANES), f32)] * (3 + 2 * len(DILATED_CONFIGS)),
        compiler_params=pltpu.CompilerParams(dimension_semantics=("parallel", "parallel"),
                                             vmem_limit_bytes=VMEM_LIMIT),
        name="dilated_attn",
    )(qp, kp, vp, bias)


def _mixer_kernel(u_ref, vln_ref, pz_ref, prev_ref, next_ref, ws_ref, bs_ref, wp_ref, sc_ref,
                  gm_ref, pool_ref, z0, z1, z2, z3, z4, *, seq):
    tm = ROW_TILE
    h = POOL_HALO
    lane = lax.broadcasted_iota(jnp.int32, (GMLP_CHUNK, GMLP_WIDTH), 1)
    for cc in range(tm // GMLP_CHUNK):
        r = slice(cc * GMLP_CHUNK, (cc + 1) * GMLP_CHUNK)
        vc = vln_ref[r, :]
        sv = jnp.zeros((GMLP_CHUNK, GMLP_WIDTH), f32)
        for g in range(GMLP_GROUPS):
            svg = jnp.dot(ws_ref[g], vc, preferred_element_type=f32)
            sv = jnp.where((lane >= g * HEAD_DIM) & (lane < (g + 1) * HEAD_DIM), svg, sv)
        gm_ref[r, :] = (u_ref[r, :] * (sv + bs_ref[...])).astype(bf16)

    i = pl.program_id(0)
    tiles_per_seq = seq // tm
    pos0 = (i % tiles_per_seq) * tm
    first = pos0 == 0
    last = pos0 + tm == seq
    z0[0:h, :] = jnp.where(first, 0.0, prev_ref[...])
    z0[h:h + tm, :] = pz_ref[...]
    z0[h + tm:h + tm + h, :] = jnp.where(last, 0.0, next_ref[...])
    z1[8:tm + 2 * h - 8, :] = z0[8:tm + 2 * h - 8, :] + z0[7:tm + 2 * h - 9, :]
    z2[16:tm + 2 * h - 16, :] = z1[15:tm + 2 * h - 17, :] + z1[17:tm + 2 * h - 15, :]
    z3[24:tm + 2 * h - 24, :] = z2[22:tm + 2 * h - 26, :] + z2[26:tm + 2 * h - 22, :]
    z4[h:h + tm, :] = z3[h - 4:h + tm - 4, :] + z3[h + 4:h + tm + 4, :]
    lane_t = lax.broadcasted_iota(jnp.int32, (tm, POOL_WIDTH), 1)
    pos = pos0 + lax.broadcasted_iota(jnp.int32, (tm, POOL_WIDTH), 0)
    zc = z0[h:h + tm, :]
    pooled = jnp.zeros((tm, POOL_WIDTH), f32)
    for g, (w, zw) in enumerate(zip(POOL_WINDOWS, (z1, z2, z3, z4))):
        left = w // 2
        right = w - 1 - left
        lo = jnp.maximum(pos - left, 0)
        hi = jnp.minimum(pos + right + 1, seq)
        cnt = (hi - lo).astype(f32)
        val = zw[h:h + tm, :] / cnt - zc
        pooled = jnp.where((lane_t >= g * HEAD_DIM) & (lane_t < (g + 1) * HEAD_DIM), val, pooled)
    y = jnp.dot(pooled.astype(bf16), wp_ref[...], preferred_element_type=f32)
    pool_ref[...] = (y * sc_ref[...]).astype(bf16)


def _mixer(u, vln, pz, ws, bs_full, wp_bd, scale, seq):
    t = u.shape[0]
    tm, h = ROW_TILE, POOL_HALO
    nt = t // tm
    hb = tm // h
    row = lambda w: pl.BlockSpec((tm, w), lambda i: (i, 0))
    const = lambda shape: pl.BlockSpec(shape, lambda i: (0,) * len(shape))
    prev_spec = pl.BlockSpec((h, POOL_WIDTH), lambda i: (jnp.maximum(i * hb - 1, 0), 0))
    next_spec = pl.BlockSpec((h, POOL_WIDTH), lambda i: (jnp.minimum((i + 1) * hb, t // h - 1), 0))
    zbuf = pltpu.VMEM((tm + 2 * h, POOL_WIDTH), f32)
    return pl.pallas_call(
        functools.partial(_mixer_kernel, seq=seq),
        grid=(nt,),
        in_specs=[row(GMLP_WIDTH), row(GMLP_WIDTH), row(POOL_WIDTH), prev_spec, next_spec,
                  const((GMLP_GROUPS, GMLP_CHUNK, GMLP_CHUNK)), const((GMLP_CHUNK, GMLP_WIDTH)),
                  const((POOL_WIDTH, POOL_WIDTH)), const((1, POOL_WIDTH))],
        out_specs=[row(GMLP_WIDTH), row(POOL_WIDTH)],
        out_shape=[jax.ShapeDtypeStruct((t, GMLP_WIDTH), bf16), jax.ShapeDtypeStruct((t, POOL_WIDTH), bf16)],
        scratch_shapes=[zbuf] * 5,
        compiler_params=pltpu.CompilerParams(dimension_semantics=("parallel",), vmem_limit_bytes=VMEM_LIMIT),
        name="mixer",
    )(u, vln, pz, pz, pz, ws, bs_full, wp_bd, scale)


def _top2_of4(b, s):
    v1, i1, s1 = b[0], jnp.zeros(b[0].shape, jnp.int32), s[0]
    for j in range(1, 4):
        gt = b[j] > v1
        v1 = jnp.where(gt, b[j], v1)
        i1 = jnp.where(gt, j, i1)
        s1 = jnp.where(gt, s[j], s1)
    v2 = jnp.full(b[0].shape, -jnp.inf, f32)
    i2 = jnp.full(b[0].shape, -1, jnp.int32)
    s2 = jnp.zeros(b[0].shape, f32)
    for j in range(4):
        cand = jnp.where(i1 != j, b[j], -jnp.inf) > v2
        v2 = jnp.where(cand, b[j], v2)
        i2 = jnp.where(cand, j, i2)
        s2 = jnp.where(cand, s[j], s2)
    return v1 + v2, i1, i2, s1, s2


def _outproj_kernel(attn_ref, gm_ref, pool_ref, x_ref, w_ref, g_ref, b_ref, rw_ref, rb_ref, tri_ref,
                    xe_ref, cnt_ref, carry):
    tm = ROW_TILE

    @pl.when(pl.program_id(0) == 0)
    def _():
        carry[...] = jnp.zeros_like(carry)

    o1, o2 = ATTN_WIDTH, ATTN_WIDTH + GMLP_WIDTH
    hmix = jnp.dot(attn_ref[...], w_ref[0:o1, :], preferred_element_type=f32)
    hmix = hmix + jnp.dot(gm_ref[...], w_ref[o1:o2, :], preferred_element_type=f32)
    hmix = hmix + jnp.dot(pool_ref[...], w_ref[o2:, :], preferred_element_type=f32)
    x1 = _layernorm(DEEPNORM_ALPHA * x_ref[...] + hmix, g_ref[...], b_ref[...])
    xe_ref[:, 0:D_MODEL] = x1

    logits = lax.dot_general(rw_ref[...], x1, (((1,), (1,)), ((), ())),
                             precision=lax.Precision.HIGHEST, preferred_element_type=f32)
    scores = 1.0 / (1.0 + jnp.exp(-logits))
    biased = scores + rb_ref[...]
    brow = [biased[e:e + 1, :] for e in range(N_EXPERTS)]
    srow = [scores[e:e + 1, :] for e in range(N_EXPERTS)]
    best = None
    for g in range(N_EXPERT_GROUPS):
        sl = slice(g * EXPERTS_PER_GROUP, (g + 1) * EXPERTS_PER_GROUP)
        gs, i1, i2, s1, s2 = _top2_of4(brow[sl], srow[sl])
        if best is None:
            best = (gs, jnp.zeros(gs.shape, jnp.int32), i1, i2, s1, s2)
        else:
            gt = gs > best[0]
            best = (jnp.where(gt, gs, best[0]), jnp.where(gt, g, best[1]), jnp.where(gt, i1, best[2]),
                    jnp.where(gt, i2, best[3]), jnp.where(gt, s1, best[4]), jnp.where(gt, s2, best[5]))
    _, gsel, i1, i2, s1, s2 = best
    den = s1 + s2
    ga, gb = s1 / den, s2 / den
    first_lo = i1 < i2
    lo = jnp.minimum(i1, i2)
    hi = jnp.maximum(i1, i2)
    g_lo = jnp.where(first_lo, ga, gb)
    g_hi = jnp.where(first_lo, gb, ga)
    pidx = jnp.where(lo == 0, hi - 1, jnp.where(lo == 1, hi + 1, N_PAIRS - 1))
    cls = gsel * N_PAIRS + pidx

    onehot = lax.broadcasted_iota(jnp.int32, (CLASS_ROWS, tm), 0) == cls
    ohf = jnp.where(onehot, 1.0, 0.0)
    prefix = jnp.dot(ohf.astype(bf16), tri_ref[...], preferred_element_type=f32)
    base = carry[:, 0:1]
    rank = jnp.sum(ohf * (prefix + base), axis=0, keepdims=True)
    new_carry = carry[...] + jnp.sum(ohf, axis=1, keepdims=True)
    carry[...] = new_carry
    cnt_ref[...] = new_carry

    srow_id = lax.broadcasted_iota(jnp.int32, (8, tm), 0)
    side8 = jnp.where(srow_id == 0, g_lo, jnp.where(srow_id == 1, g_hi, jnp.where(
        srow_id == 2, cls.astype(f32), jnp.where(srow_id == 3, rank, 0.0))))
    side = jnp.concatenate([side8, jnp.zeros((SIDE - 8, tm), f32)], axis=0)
    xe_ref[:, D_MODEL:] = side.T


def _outproj(attn, gm, pool, x, w_out, g, b, rw_t, rb, tri):
    t = x.shape[0]
    tm = ROW_TILE
    row = lambda w: pl.BlockSpec((tm, w), lambda i: (i, 0))
    const = lambda shape: pl.BlockSpec(shape, lambda i: (0,) * len(shape))
    return pl.pallas_call(
        _outproj_kernel,
        grid=(t // tm,),
        in_specs=[row(ATTN_WIDTH), row(GMLP_WIDTH), row(POOL_WIDTH), row(D_MODEL), const((D_MODEL, D_MODEL)),
                  const((1, D_MODEL)), const((1, D_MODEL)), const((N_EXPERTS, D_MODEL)), const((N_EXPERTS, 1)),
                  const((tm, tm))],
        out_specs=[row(D_MODEL + SIDE), const((CLASS_ROWS, LANES))],
        out_shape=[jax.ShapeDtypeStruct((t, D_MODEL + SIDE), f32), jax.ShapeDtypeStruct((CLASS_ROWS, LANES), f32)],
        scratch_shapes=[pltpu.VMEM((CLASS_ROWS, LANES), f32)],
        compiler_params=pltpu.CompilerParams(dimension_semantics=("arbitrary",), vmem_limit_bytes=VMEM_LIMIT),
        name="outproj_ln_router",
    )(attn, gm, pool, x, w_out, g, b, rw_t, rb, tri)


def _permute_kernel(idx_ref, zflag_ref, src_ref, *rest, scatter, nzero, width):
    if scatter:
        dst_ref, zeros, sem, zsem = rest
    else:
        dst_ref, sem = rest
    step = pl.program_id(0)

    if scatter:
        def zcopy(j):
            return pltpu.make_async_copy(zeros, dst_ref.at[pl.ds(pl.multiple_of(j * MOE_BLK, MOE_BLK), MOE_BLK), :],
                                         zsem)

        @pl.when(step == 0)
        def _():
            zeros[...] = jnp.zeros_like(zeros)

            def zstart(j, c):
                @pl.when(zflag_ref[j] != 0)
                def _():
                    zcopy(j).start()
                return c

            def zwait(j, c):
                @pl.when(zflag_ref[j] != 0)
                def _():
                    zcopy(j).wait()
                return c

            lax.fori_loop(0, nzero, zstart, 0)
            lax.fori_loop(0, nzero, zwait, 0)

    def row_copy(r):
        other = idx_ref[0, 0, r]
        if scatter:
            return pltpu.make_async_copy(src_ref.at[pl.ds(r, 1), :], dst_ref.at[pl.ds(other, 1), :], sem)
        return pltpu.make_async_copy(src_ref.at[pl.ds(other, 1), :], dst_ref.at[pl.ds(r, 1), :], sem)

    def start(r, c):
        row_copy(r).start()
        return c

    def wait(r, c):
        row_copy(r).wait()
        return c

    lax.fori_loop(0, PERM_CHUNK, start, 0)
    lax.fori_loop(0, PERM_CHUNK, wait, 0)


def _permute(idx, zflag, src, n_dst, width, scatter):
    t = idx.shape[0]
    nsteps = t // PERM_CHUNK
    idx3 = idx.reshape(nsteps, 1, PERM_CHUNK)
    nzero = zflag.shape[0]
    scratch = [pltpu.SemaphoreType.DMA(())]
    if scatter:
        scratch = [pltpu.VMEM((MOE_BLK, width), f32), pltpu.SemaphoreType.DMA(()), pltpu.SemaphoreType.DMA(())]
    block = pl.BlockSpec((PERM_CHUNK, width), lambda i: (i, 0))
    hbm = pl.BlockSpec(memory_space=pl.ANY)
    return pl.pallas_call(
        functools.partial(_permute_kernel, scatter=scatter, nzero=nzero, width=width),
        grid=(nsteps,),
        in_specs=[pl.BlockSpec((1, 1, PERM_CHUNK), lambda i: (i, 0, 0), memory_space=pltpu.SMEM),
                  pl.BlockSpec(memory_space=pltpu.SMEM),
                  block if scatter else hbm],
        out_specs=hbm if scatter else block,
        out_shape=jax.ShapeDtypeStruct((n_dst, width), f32),
        scratch_shapes=scratch,
        compiler_params=pltpu.CompilerParams(dimension_semantics=("arbitrary",), vmem_limit_bytes=VMEM_LIMIT),
        name="dispatch_rows" if scatter else "undispatch_rows",
    )(idx3, zflag, src)


def _expert_kernel(elo_ref, ehi_ref, nused_ref, xb_ref, wg_lo, wu_lo, wd_lo, wg_hi, wu_hi, wd_hi, g_ref, b_ref,
                   out_ref):
    j = pl.program_id(0)

    @pl.when(j < nused_ref[0])
    def _():
        x = xb_ref[:, 0:D_MODEL]
        gates = xb_ref[:, D_MODEL:D_MODEL + SIDE]
        xb = x.astype(bf16)
        moe = jnp.zeros_like(x)
        for col, (wg, wu, wd) in enumerate(((wg_lo, wu_lo, wd_lo), (wg_hi, wu_hi, wd_hi))):
            a = jnp.dot(xb, wg[...], preferred_element_type=f32)
            up = jnp.dot(xb, wu[...], preferred_element_type=f32)
            hidden = (a / (1.0 + jnp.exp(-a))) * up
            y = jnp.dot(hidden.astype(bf16), wd[...], preferred_element_type=f32)
            moe = moe + y * gates[:, col:col + 1]
        out_ref[...] = _layernorm(DEEPNORM_ALPHA * x + moe, g_ref[...], b_ref[...])

    @pl.when(j >= nused_ref[0])
    def _():
        out_ref[...] = jnp.zeros_like(out_ref)


def _experts(elo, ehi, nused, buf, wg, wu, wd, g, b):
    p = buf.shape[0]
    nb = p // MOE_BLK
    wspec = lambda shape, which: pl.BlockSpec(
        (None,) + shape, (lambda j, elo, ehi, nu: (elo[j], 0, 0)) if which == 0 else (lambda j, elo, ehi, nu: (ehi[j], 0, 0)))
    up_shape, down_shape = (D_MODEL, EXPERT_FF), (EXPERT_FF, D_MODEL)
    const = lambda shape: pl.BlockSpec(shape, lambda j, elo, ehi, nu: (0,) * len(shape))
    grid_spec = pltpu.PrefetchScalarGridSpec(
        num_scalar_prefetch=3,
        grid=(nb,),
        in_specs=[pl.BlockSpec((MOE_BLK, D_MODEL + SIDE), lambda j, elo, ehi, nu: (j, 0)),
                  wspec(up_shape, 0), wspec(up_shape, 0), wspec(down_shape, 0),
                  wspec(up_shape, 1), wspec(up_shape, 1), wspec(down_shape, 1),
                  const((1, D_MODEL)), const((1, D_MODEL))],
        out_specs=pl.BlockSpec((MOE_BLK, D_MODEL), lambda j, elo, ehi, nu: (j, 0)),
    )
    return pl.pallas_call(
        _expert_kernel,
        grid_spec=grid_spec,
        out_shape=jax.ShapeDtypeStruct((p, D_MODEL), f32),
        compiler_params=pltpu.CompilerParams(dimension_semantics=("arbitrary",), vmem_limit_bytes=VMEM_LIMIT),
        name="expert_pairs",
    )(elo, ehi, nused, buf, wg, wu, wd, wg, wu, wd, g, b)


def _rope_tables(seq):
    half = ROPE_DIM // 2
    inv_freq = jnp.power(jnp.float32(ROPE_THETA), -jnp.arange(half, dtype=f32) / half)
    ang = jnp.arange(seq, dtype=f32)[:, None] * inv_freq[None, :]
    lane = np.arange(LANES) % HEAD_DIM
    fidx = lane % half
    in_rope = lane < ROPE_DIM
    first = lane < half
    cos_l = jnp.cos(ang)[:, fidx]
    sin_l = jnp.sin(ang)[:, fidx]
    cos_t = jnp.where(in_rope[None, :], cos_l, 1.0)
    sa_t = jnp.where((in_rope & first)[None, :], -sin_l, 0.0)
    sb_t = jnp.where((in_rope & ~first)[None, :], sin_l, 0.0)
    return cos_t, sa_t, sb_t


_PAIRS = [(a, b) for a in range(EXPERTS_PER_GROUP) for b in range(a + 1, EXPERTS_PER_GROUP)]


def _dispatch_plan(side, counts):
    t = side.shape[0]
    cls = side[:, 2].astype(jnp.int32)
    rank = side[:, 3].astype(jnp.int32)
    counts = counts.astype(jnp.int32)
    padded = ((counts + MOE_BLK - 1) // MOE_BLK) * MOE_BLK
    pend = jnp.cumsum(padded)
    pstart = pend - padded
    dest = pstart[cls] + rank
    nb = (t + N_CLASSES * MOE_BLK) // MOE_BLK
    blk_start = jnp.arange(nb, dtype=jnp.int32) * MOE_BLK
    blk_cls = jnp.minimum(jnp.sum((blk_start[:, None] >= pend[None, :]).astype(jnp.int32), axis=1), N_CLASSES - 1)
    valid_end = pstart[blk_cls] + counts[blk_cls]
    zflag = ((blk_start + MOE_BLK > valid_end) | (blk_start >= pend[-1])).astype(jnp.int32)
    pair_lo = jnp.asarray([p[0] for p in _PAIRS], jnp.int32)
    pair_hi = jnp.asarray([p[1] for p in _PAIRS], jnp.int32)
    grp = blk_cls // N_PAIRS
    elo = grp * EXPERTS_PER_GROUP + pair_lo[blk_cls % N_PAIRS]
    ehi = grp * EXPERTS_PER_GROUP + pair_hi[blk_cls % N_PAIRS]
    nused = (pend[-1] // MOE_BLK).astype(jnp.int32).reshape(1)
    return dest, zflag, elo, ehi, nused


def kernel(x, w_in, w_out, gmlp_ln_g, gmlp_ln_b, gmlp_w_s, gmlp_b_s, pool_w, pool_scale, ln1_g, ln1_b,
           router_w, router_bias, w_gate, w_up, w_down, ln2_g, ln2_b):
    batch, seq, d = x.shape
    t = batch * seq
    assert d == D_MODEL and seq % ROW_TILE == 0 and t % PERM_CHUNK == 0
    cos_t, sa_t, sb_t = _rope_tables(seq)
    tri = (np.arange(ROW_TILE)[:, None] < np.arange(ROW_TILE)[None, :])
    tri = jnp.asarray(tri, bf16)
    rw_t = router_w.T
    rb = router_bias.reshape(N_EXPERTS, 1)
    n_rows = t + N_CLASSES * MOE_BLK
    dummy_flag = jnp.zeros((1,), jnp.int32)

    xf = x.reshape(t, d)
    for l in range(DEPTH):
        lng = gmlp_ln_g[l].reshape(1, GMLP_WIDTH)
        lnb = gmlp_ln_b[l].reshape(1, GMLP_WIDTH)
        qp, kp, vp, u, vln, pz = _inproj(xf, w_in[l].astype(bf16), cos_t, sa_t, sb_t, lng, lnb, seq)
        attn = _attention(qp, kp, vp, batch, seq)
        bs_full = jnp.repeat(gmlp_b_s[l].T, HEAD_DIM, axis=1)
        wp_bd = jnp.zeros((POOL_WIDTH, POOL_WIDTH), f32)
        for g in range(len(POOL_WINDOWS)):
            wp_bd = wp_bd.at[g * HEAD_DIM:(g + 1) * HEAD_DIM, g * HEAD_DIM:(g + 1) * HEAD_DIM].set(pool_w[l, g])
        wp_bd = wp_bd.astype(bf16)
        gm, pool = _mixer(u, vln, pz, gmlp_w_s[l].astype(bf16), bs_full, wp_bd,
                          pool_scale[l].reshape(1, POOL_WIDTH), seq)
        xe, counts = _outproj(attn, gm, pool, xf, w_out[l].astype(bf16), ln1_g[l].reshape(1, d),
                              ln1_b[l].reshape(1, d), rw_t, rb, tri)
        dest, zflag, elo, ehi, nused = _dispatch_plan(xe[:, D_MODEL:D_MODEL + 4], counts[:N_CLASSES, 0])
        buf = _permute(dest, zflag, xe, n_rows, D_MODEL + SIDE, scatter=True)
        ys = _experts(elo, ehi, nused, buf, w_gate[l].astype(bf16), w_up[l].astype(bf16),
                      w_down[l].astype(bf16), ln2_g[l].reshape(1, d), ln2_b[l].reshape(1, d))
        xf = _permute(dest, dummy_flag, ys, t, D_MODEL, scatter=False)
    return xf.reshape(batch, seq, d)
```

```python
import functools

import jax
import jax.numpy as jnp
import numpy as np
from jax import lax
from jax.experimental import pallas as pl
from jax.experimental.pallas import tpu as pltpu

f32 = jnp.float32
bf16 = jnp.bfloat16

D_MODEL = 1024
HEAD_DIM = 64
ATTN_WIDTH = 512
DILATED_CONFIGS = ((128, 1), (512, 4), (2048, 16))
ATTN_HALF = 64
ROPE_THETA = 500000.0
ROPE_DIM = 16
GMLP_WIDTH = 256
GMLP_GROUPS = 4
GMLP_CHUNK = 128
POOL_WIDTH = 256
POOL_WINDOWS = (2, 4, 8, 16)
IN_COLS = 3 * ATTN_WIDTH + 2 * GMLP_WIDTH + POOL_WIDTH
N_EXPERTS = 16
N_EXPERT_GROUPS = 4
EXPERTS_PER_GROUP = 4
EXPERT_FF = 512
DEPTH = 2
DEEPNORM_ALPHA = float((2 * DEPTH) ** 0.25)
LN_EPS = 1e-5
NEG_INF = -1e30

LANES = 128
ROW_TILE = 512
ATTN_QBLK = 128
POOL_HALO = 32
N_PAIRS = 6
N_CLASSES = N_EXPERT_GROUPS * N_PAIRS
CLASS_ROWS = 32
MOE_BLK = 256
SIDE = LANES
PERM_CHUNK = ROW_TILE
PERM_UNROLL = 8
VMEM_LIMIT = 48 * 1024 * 1024


def _layernorm(y, g, b):
    mu = jnp.mean(y, axis=-1, keepdims=True)
    d = y - mu
    var = jnp.mean(d * d, axis=-1, keepdims=True)
    return d * lax.rsqrt(var + LN_EPS) * g + b


def _group_mean(v):
    lane = lax.broadcasted_iota(jnp.int32, v.shape, 1)
    out = jnp.zeros_like(v)
    for g in range(GMLP_GROUPS):
        m = (lane >= g * HEAD_DIM) & (lane < (g + 1) * HEAD_DIM)
        s = jnp.sum(jnp.where(m, v, 0.0), axis=-1, keepdims=True)
        out = jnp.where(m, s, out)
    return out * (1.0 / HEAD_DIM)


def _inproj_kernel(x_ref, w_ref, cos_ref, sa_ref, sb_ref, lng_ref, lnb_ref,
                   q_ref, k_ref, v_ref, u_ref, vln_ref, pz_ref):
    xb = x_ref[...].astype(bf16)
    cos = cos_ref[...]
    sa = sa_ref[...]
    sb = sb_ref[...]
    for base, out_ref, rope, scale in ((0, q_ref, True, HEAD_DIM ** -0.5),
                                       (ATTN_WIDTH, k_ref, True, 1.0),
                                       (2 * ATTN_WIDTH, v_ref, False, 1.0)):
        y = jnp.dot(xb, w_ref[:, base:base + ATTN_WIDTH], preferred_element_type=f32)
        for p in range(ATTN_WIDTH // LANES):
            yp = y[:, p * LANES:(p + 1) * LANES]
            if rope:
                yp = yp * cos + pltpu.roll(yp, LANES - ROPE_DIM // 2, 1) * sa + pltpu.roll(yp, ROPE_DIM // 2, 1) * sb
            if scale != 1.0:
                yp = yp * scale
            out_ref[p] = yp.astype(bf16)
    o3 = 3 * ATTN_WIDTH
    y = jnp.dot(xb, w_ref[:, o3:o3 + 3 * GMLP_WIDTH], preferred_element_type=f32)
    u_ref[...] = jax.nn.gelu(y[:, :GMLP_WIDTH])
    gv = jax.nn.gelu(y[:, GMLP_WIDTH:2 * GMLP_WIDTH])
    mu = _group_mean(gv)
    d = gv - mu
    var = _group_mean(d * d)
    vln_ref[...] = (d * lax.rsqrt(var + LN_EPS) * lng_ref[...] + lnb_ref[...]).astype(bf16)
    pz_ref[...] = y[:, 2 * GMLP_WIDTH:]


def _inproj(x, w_in, cos_t, sa_t, sb_t, lng, lnb, seq):
    t = x.shape[0]
    nt = t // ROW_TILE
    tiles_per_seq = seq // ROW_TILE
    npair = ATTN_WIDTH // LANES
    tab_spec = pl.BlockSpec((ROW_TILE, LANES), lambda i: (i % tiles_per_seq, 0))
    row = lambda w: pl.BlockSpec((ROW_TILE, w), lambda i: (i, 0))
    const = lambda shape: pl.BlockSpec(shape, lambda i: (0,) * len(shape))
    pair_spec = pl.BlockSpec((npair, ROW_TILE, LANES), lambda i: (0, i, 0))
    pair_shape = jax.ShapeDtypeStruct((npair, t, LANES), bf16)
    return pl.pallas_call(
        _inproj_kernel,
        grid=(nt,),
        in_specs=[row(D_MODEL), const((D_MODEL, IN_COLS)), tab_spec, tab_spec, tab_spec,
                  const((1, GMLP_WIDTH)), const((1, GMLP_WIDTH))],
        out_specs=[pair_spec, pair_spec, pair_spec, row(GMLP_WIDTH), row(GMLP_WIDTH), row(POOL_WIDTH)],
        out_shape=[pair_shape, pair_shape, pair_shape,
                   jax.ShapeDtypeStruct((t, GMLP_WIDTH), f32),
                   jax.ShapeDtypeStruct((t, GMLP_WIDTH), bf16),
                   jax.ShapeDtypeStruct((t, POOL_WIDTH), f32)],
        compiler_params=pltpu.CompilerParams(dimension_semantics=("parallel",), vmem_limit_bytes=VMEM_LIMIT),
        name="inproj",
    )(x, w_in, cos_t, sa_t, sb_t, lng, lnb)


def _attn_kernel(q_ref, k_ref, v_ref, bias_ref, o_ref, qs, ks, vs, o0, o1, o2, l0s, l1s, l2s, *, seq):
    os_ = (o0, o1, o2)
    ls_ = (l0s, l1s, l2s)
    qs[...] = q_ref[...].astype(f32)
    ks[...] = k_ref[...].astype(f32)
    vs[...] = v_ref[...].astype(f32)
    lane = lax.broadcasted_iota(jnp.int32, (ATTN_QBLK, LANES), 1)
    head_a = lane < HEAD_DIM

    for ci, (_, dil) in enumerate(DILATED_CONFIGS):
        length = seq // dil
        nblk = length // ATTN_QBLK
        win = min(2 * ATTN_QBLK, length)
        shift = dil.bit_length() - 1

        def rows(start, size, dil=dil):
            if dil == 1:
                return pl.ds(pl.multiple_of(start, 8), size)
            return pl.ds(start, size, stride=dil)

        def body(u, carry, ci=ci, dil=dil, length=length, win=win, shift=shift, rows=rows):
            c = u & (dil - 1)
            i = u >> shift
            k0 = jnp.clip(i * ATTN_QBLK - ATTN_HALF, 0, length - win)
            delta = k0 - i * ATTN_QBLK
            qrows = rows(c + dil * ATTN_QBLK * i, ATTN_QBLK)
            krows = rows(c + dil * k0, win)
            q = qs[qrows, :]
            kb = ks[krows, :].astype(bf16)
            vb = vs[krows, :].astype(bf16)
            bias = bias_ref[lax.shift_right_logical(-delta, ATTN_HALF.bit_length() - 1), :, 0:win]
            q2 = jnp.concatenate([jnp.where(head_a, q, 0.0), jnp.where(head_a, 0.0, q)], axis=0).astype(bf16)
            s = lax.dot_general(q2, kb, (((1,), (1,)), ((), ())), preferred_element_type=f32) + bias
            m = jnp.max(s, axis=-1, keepdims=True)
            p = jnp.exp(s - m).astype(bf16)
            ol = jnp.dot(p, jnp.concatenate([vb, jnp.ones_like(vb)], axis=1), preferred_element_type=f32)
            l = ol[:, LANES:]
            o = ol[:, :LANES] / l
            lse = m + jnp.log(l)
            os_[ci][qrows, :] = jnp.where(head_a, o[:ATTN_QBLK], o[ATTN_QBLK:])
            ls_[ci][qrows, :] = jnp.where(head_a, lse[:ATTN_QBLK], lse[ATTN_QBLK:])
            return carry

        lax.fori_loop(0, dil * nblk, body, 0, unroll=8)

    chunk = 256

    def combine(j, carry):
        r = pl.ds(pl.multiple_of(j * chunk, chunk), chunk)
        l0, l1, l2 = l0s[r, :], l1s[r, :], l2s[r, :]
        mx = jnp.maximum(jnp.maximum(l0, l1), l2)
        w0, w1, w2 = jnp.exp(l0 - mx), jnp.exp(l1 - mx), jnp.exp(l2 - mx)
        num = w0 * o0[r, :] + w1 * o1[r, :] + w2 * o2[r, :]
        o_ref[r, :] = (num / (w0 + w1 + w2)).astype(bf16)
        return carry

    lax.fori_loop(0, seq // chunk, combine, 0)


def _attention(qp, kp, vp, batch, seq):
    npair = qp.shape[0]
    t = batch * seq
    in_spec = pl.BlockSpec((None, seq, LANES), lambda p, b: (p, b, 0))
    win = 2 * ATTN_QBLK
    rr = np.arange(2 * ATTN_QBLK)[None, :, None] % ATTN_QBLK
    cc = np.arange(win)[None, None, :]
    dd = np.arange(3)[:, None, None] * ATTN_HALF
    bias = jnp.asarray(np.where(np.abs(cc - dd - rr) <= ATTN_HALF, 0.0, NEG_INF), f32)
    return pl.pallas_call(
        functools.partial(_attn_kernel, seq=seq),
        grid=(npair, batch),
        in_specs=[in_spec, in_spec, in_spec, pl.BlockSpec(bias.shape, lambda p, b: (0, 0, 0))],
        out_specs=pl.BlockSpec((seq, LANES), lambda p, b: (b, p)),
        out_shape=jax.ShapeDtypeStruct((t, ATTN_WIDTH), bf16),
        scratch_shapes=[pltpu.VMEM((seq, LANES), f32)] * (3 + 2 * len(DILATED_CONFIGS)),
        compiler_params=pltpu.CompilerParams(dimension_semantics=("parallel", "parallel"),
                                             vmem_limit_bytes=VMEM_LIMIT),
        name="dilated_attn",
    )(qp, kp, vp, bias)


def _mixer_kernel(u_ref, vln_ref, pz_ref, prev_ref, next_ref, ws_ref, bs_ref, wp_ref, sc_ref,
                  gm_ref, pool_ref, z0, z1, z2, z3, z4, *, seq):
    tm = ROW_TILE
    h = POOL_HALO
    lane = lax.broadcasted_iota(jnp.int32, (GMLP_CHUNK, GMLP_WIDTH), 1)
    for cc in range(tm // GMLP_CHUNK):
        r = slice(cc * GMLP_CHUNK, (cc + 1) * GMLP_CHUNK)
        vc = vln_ref[r, :]
        sv = jnp.zeros((GMLP_CHUNK, GMLP_WIDTH), f32)
        for g in range(GMLP_GROUPS):
            svg = jnp.dot(ws_ref[g], vc, preferred_element_type=f32)
            sv = jnp.where((lane >= g * HEAD_DIM) & (lane < (g + 1) * HEAD_DIM), svg, sv)
        gm_ref[r, :] = (u_ref[r, :] * (sv + bs_ref[...])).astype(bf16)

    i = pl.program_id(0)
    tiles_per_seq = seq // tm
    pos0 = (i % tiles_per_seq) * tm
    first = pos0 == 0
    last = pos0 + tm == seq
    z0[0:h, :] = jnp.where(first, 0.0, prev_ref[...])
    z0[h:h + tm, :] = pz_ref[...]
    z0[h + tm:h + tm + h, :] = jnp.where(last, 0.0, next_ref[...])
    z1[8:tm + 2 * h - 8, :] = z0[8:tm + 2 * h - 8, :] + z0[7:tm + 2 * h - 9, :]
    z2[16:tm + 2 * h - 16, :] = z1[15:tm + 2 * h - 17, :] + z1[17:tm + 2 * h - 15, :]
    z3[24:tm + 2 * h - 24, :] = z2[22:tm + 2 * h - 26, :] + z2[26:tm + 2 * h - 22, :]
    z4[h:h + tm, :] = z3[h - 4:h + tm - 4, :] + z3[h + 4:h + tm + 4, :]
    lane_t = lax.broadcasted_iota(jnp.int32, (tm, POOL_WIDTH), 1)
    pos = pos0 + lax.broadcasted_iota(jnp.int32, (tm, POOL_WIDTH), 0)
    zc = z0[h:h + tm, :]
    pooled = jnp.zeros((tm, POOL_WIDTH), f32)
    for g, (w, zw) in enumerate(zip(POOL_WINDOWS, (z1, z2, z3, z4))):
        left = w // 2
        right = w - 1 - left
        lo = jnp.maximum(pos - left, 0)
        hi = jnp.minimum(pos + right + 1, seq)
        cnt = (hi - lo).astype(f32)
        val = zw[h:h + tm, :] / cnt - zc
        pooled = jnp.where((lane_t >= g * HEAD_DIM) & (lane_t < (g + 1) * HEAD_DIM), val, pooled)
    y = jnp.dot(pooled.astype(bf16), wp_ref[...], preferred_element_type=f32)
    pool_ref[...] = (y * sc_ref[...]).astype(bf16)


def _mixer(u, vln, pz, ws, bs_full, wp_bd, scale, seq):
    t = u.shape[0]
    tm, h = ROW_TILE, POOL_HALO
    nt = t // tm
    hb = tm // h
    row = lambda w: pl.BlockSpec((tm, w), lambda i: (i, 0))
    const = lambda shape: pl.BlockSpec(shape, lambda i: (0,) * len(shape))
    prev_spec = pl.BlockSpec((h, POOL_WIDTH), lambda i: (jnp.maximum(i * hb - 1, 0), 0))
    next_spec = pl.BlockSpec((h, POOL_WIDTH), lambda i: (jnp.minimum((i + 1) * hb, t // h - 1), 0))
    zbuf = pltpu.VMEM((tm + 2 * h, POOL_WIDTH), f32)
    return pl.pallas_call(
        functools.partial(_mixer_kernel, seq=seq),
        grid=(nt,),
        in_specs=[row(GMLP_WIDTH), row(GMLP_WIDTH), row(POOL_WIDTH), prev_spec, next_spec,
                  const((GMLP_GROUPS, GMLP_CHUNK, GMLP_CHUNK)), const((GMLP_CHUNK, GMLP_WIDTH)),
                  const((POOL_WIDTH, POOL_WIDTH)), const((1, POOL_WIDTH))],
        out_specs=[row(GMLP_WIDTH), row(POOL_WIDTH)],
        out_shape=[jax.ShapeDtypeStruct((t, GMLP_WIDTH), bf16), jax.ShapeDtypeStruct((t, POOL_WIDTH), bf16)],
        scratch_shapes=[zbuf] * 5,
        compiler_params=pltpu.CompilerParams(dimension_semantics=("parallel",), vmem_limit_bytes=VMEM_LIMIT),
        name="mixer",
    )(u, vln, pz, pz, pz, ws, bs_full, wp_bd, scale)


def _top2_of4(b, s):
    v1, i1, s1 = b[0], jnp.zeros(b[0].shape, jnp.int32), s[0]
    for j in range(1, 4):
        gt = b[j] > v1
        v1 = jnp.where(gt, b[j], v1)
        i1 = jnp.where(gt, j, i1)
        s1 = jnp.where(gt, s[j], s1)
    v2 = jnp.full(b[0].shape, -jnp.inf, f32)
    i2 = jnp.full(b[0].shape, -1, jnp.int32)
    s2 = jnp.zeros(b[0].shape, f32)
    for j in range(4):
        cand = jnp.where(i1 != j, b[j], -jnp.inf) > v2
        v2 = jnp.where(cand, b[j], v2)
        i2 = jnp.where(cand, j, i2)
        s2 = jnp.where(cand, s[j], s2)
    return v1 + v2, i1, i2, s1, s2


def _outproj_kernel(attn_ref, gm_ref, pool_ref, x_ref, w_ref, g_ref, b_ref, rw_ref, rb_ref, tri_ref,
                    xe_ref, meta_ref, cnt_ref, carry):
    tm = ROW_TILE

    @pl.when(pl.program_id(0) == 0)
    def _():
        carry[...] = jnp.zeros_like(carry)

    o1, o2 = ATTN_WIDTH, ATTN_WIDTH + GMLP_WIDTH
    hmix = jnp.dot(attn_ref[...], w_ref[0:o1, :], preferred_element_type=f32)
    hmix = hmix + jnp.dot(gm_ref[...], w_ref[o1:o2, :], preferred_element_type=f32)
    hmix = hmix + jnp.dot(pool_ref[...], w_ref[o2:, :], preferred_element_type=f32)
    x1 = _layernorm(DEEPNORM_ALPHA * x_ref[...] + hmix, g_ref[...], b_ref[...])
    xe_ref[:, 0:D_MODEL] = x1

    logits = lax.dot_general(rw_ref[...], x1.astype(bf16), (((1,), (1,)), ((), ())),
                             preferred_element_type=f32)
    scores = 1.0 / (1.0 + jnp.exp(-logits))
    biased = scores + rb_ref[...]
    brow = [biased[e:e + 1, :] for e in range(N_EXPERTS)]
    srow = [scores[e:e + 1, :] for e in range(N_EXPERTS)]
    best = None
    for g in range(N_EXPERT_GROUPS):
        sl = slice(g * EXPERTS_PER_GROUP, (g + 1) * EXPERTS_PER_GROUP)
        gs, i1, i2, s1, s2 = _top2_of4(brow[sl], srow[sl])
        if best is None:
            best = (gs, jnp.zeros(gs.shape, jnp.int32), i1, i2, s1, s2)
        else:
            gt = gs > best[0]
            best = (jnp.where(gt, gs, best[0]), jnp.where(gt, g, best[1]), jnp.where(gt, i1, best[2]),
                    jnp.where(gt, i2, best[3]), jnp.where(gt, s1, best[4]), jnp.where(gt, s2, best[5]))
    _, gsel, i1, i2, s1, s2 = best
    den = s1 + s2
    ga, gb = s1 / den, s2 / den
    first_lo = i1 < i2
    lo = jnp.minimum(i1, i2)
    hi = jnp.maximum(i1, i2)
    g_lo = jnp.where(first_lo, ga, gb)
    g_hi = jnp.where(first_lo, gb, ga)
    pidx = jnp.where(lo == 0, hi - 1, jnp.where(lo == 1, hi + 1, N_PAIRS - 1))
    cls = gsel * N_PAIRS + pidx

    onehot = lax.broadcasted_iota(jnp.int32, (CLASS_ROWS, tm), 0) == cls
    ohf = jnp.where(onehot, 1.0, 0.0)
    prefix = jnp.dot(ohf.astype(bf16), tri_ref[...], preferred_element_type=f32)
    base = carry[:, 0:1]
    rank = jnp.sum(ohf * (prefix + base), axis=0, keepdims=True)
    new_carry = carry[...] + jnp.sum(ohf, axis=1, keepdims=True)
    carry[...] = new_carry
    cnt_ref[...] = new_carry

    srow_id = lax.broadcasted_iota(jnp.int32, (8, tm), 0)
    meta_ref[...] = jnp.where(srow_id == 0, cls, jnp.where(srow_id == 1, rank.astype(jnp.int32), 0))
    side8 = jnp.where(srow_id == 0, g_lo, jnp.where(srow_id == 1, g_hi, 0.0))
    side = jnp.concatenate([side8, jnp.zeros((SIDE - 8, tm), f32)], axis=0)
    xe_ref[:, D_MODEL:] = side.T


def _outproj(attn, gm, pool, x, w_out, g, b, rw_pad, rb, tri):
    t = x.shape[0]
    tm = ROW_TILE
    row = lambda w: pl.BlockSpec((tm, w), lambda i: (i, 0))
    const = lambda shape: pl.BlockSpec(shape, lambda i: (0,) * len(shape))
    return pl.pallas_call(
        _outproj_kernel,
        grid=(t // tm,),
        in_specs=[row(ATTN_WIDTH), row(GMLP_WIDTH), row(POOL_WIDTH), row(D_MODEL), const((D_MODEL, D_MODEL)),
                  const((1, D_MODEL)), const((1, D_MODEL)), const((N_EXPERTS, D_MODEL)), const((N_EXPERTS, 1)),
                  const((tm, tm))],
        out_specs=[row(D_MODEL + SIDE), pl.BlockSpec((None, 8, tm), lambda i: (i, 0, 0)),
                   const((CLASS_ROWS, LANES))],
        out_shape=[jax.ShapeDtypeStruct((t, D_MODEL + SIDE), f32), jax.ShapeDtypeStruct((t // tm, 8, tm), jnp.int32),
                   jax.ShapeDtypeStruct((CLASS_ROWS, LANES), f32)],
        scratch_shapes=[pltpu.VMEM((CLASS_ROWS, LANES), f32)],
        compiler_params=pltpu.CompilerParams(dimension_semantics=("arbitrary",), vmem_limit_bytes=VMEM_LIMIT),
        name="outproj_ln_router",
    )(attn, gm, pool, x, w_out, g, b, rw_pad, rb, tri)


def _permute_kernel(meta_ref, pstart_ref, zflag_ref, src_ref, *rest, scatter, nzero):
    if scatter:
        dst_ref, zeros, sem, zsem = rest
    else:
        dst_ref, sem = rest
    step = pl.program_id(0)

    if scatter:
        def zcopy(j):
            return pltpu.make_async_copy(zeros, dst_ref.at[pl.ds(pl.multiple_of(j * MOE_BLK, MOE_BLK), MOE_BLK), :],
                                         zsem)

        @pl.when(step == 0)
        def _():
            zeros[...] = jnp.zeros_like(zeros)

            def zstart(j, c):
                @pl.when(zflag_ref[j] != 0)
                def _():
                    zcopy(j).start()
                return c

            def zwait(j, c):
                @pl.when(zflag_ref[j] != 0)
                def _():
                    zcopy(j).wait()
                return c

            lax.fori_loop(0, nzero, zstart, 0)
            lax.fori_loop(0, nzero, zwait, 0)

    def row_copy(r):
        other = pstart_ref[meta_ref[0, 0, r]] + meta_ref[0, 1, r]
        if scatter:
            return pltpu.make_async_copy(src_ref.at[pl.ds(r, 1), :], dst_ref.at[pl.ds(other, 1), :], sem)
        return pltpu.make_async_copy(src_ref.at[pl.ds(other, 1), :], dst_ref.at[pl.ds(r, 1), :], sem)

    def start(g, c):
        for k in range(PERM_UNROLL):
            row_copy(g * PERM_UNROLL + k).start(priority=k % 2)
        return c

    lax.fori_loop(0, PERM_CHUNK // PERM_UNROLL, start, 0)
    if scatter:
        pltpu.make_async_copy(src_ref, dst_ref.at[pl.ds(0, PERM_CHUNK), :], sem).wait()
    else:
        pltpu.make_async_copy(src_ref.at[pl.ds(0, PERM_CHUNK), :], dst_ref, sem).wait()


def _permute(meta, pstart, zflag, src, n_dst, width, scatter):
    nsteps = meta.shape[0]
    nzero = zflag.shape[0]
    scratch = [pltpu.SemaphoreType.DMA(())]
    if scatter:
        scratch = [pltpu.VMEM((MOE_BLK, width), f32), pltpu.SemaphoreType.DMA(()), pltpu.SemaphoreType.DMA(())]
    block = pl.BlockSpec((PERM_CHUNK, width), lambda i: (i, 0))
    hbm = pl.BlockSpec(memory_space=pl.ANY)
    smem = pl.BlockSpec(memory_space=pltpu.SMEM)
    return pl.pallas_call(
        functools.partial(_permute_kernel, scatter=scatter, nzero=nzero),
        grid=(nsteps,),
        in_specs=[pl.BlockSpec((1, 8, PERM_CHUNK), lambda i: (i, 0, 0), memory_space=pltpu.SMEM), smem, smem,
                  block if scatter else hbm],
        out_specs=hbm if scatter else block,
        out_shape=jax.ShapeDtypeStruct((n_dst, width), f32),
        scratch_shapes=scratch,
        compiler_params=pltpu.CompilerParams(dimension_semantics=("arbitrary",), vmem_limit_bytes=VMEM_LIMIT),
        name="dispatch_rows" if scatter else "undispatch_rows",
    )(meta, pstart, zflag, src)


def _expert_kernel(elo_ref, ehi_ref, nused_ref, xb_ref, wg_lo, wu_lo, wd_lo, wg_hi, wu_hi, wd_hi, g_ref, b_ref,
                   out_ref):
    j = pl.program_id(0)

    @pl.when(j < nused_ref[0])
    def _():
        x = xb_ref[:, 0:D_MODEL]
        gates = xb_ref[:, D_MODEL:D_MODEL + SIDE]
        xb = x.astype(bf16)
        moe = jnp.zeros_like(x)
        for col, (wg, wu, wd) in enumerate(((wg_lo, wu_lo, wd_lo), (wg_hi, wu_hi, wd_hi))):
            a = jnp.dot(xb, wg[...], preferred_element_type=f32)
            up = jnp.dot(xb, wu[...], preferred_element_type=f32)
            hidden = (a / (1.0 + jnp.exp(-a))) * up
            y = jnp.dot(hidden.astype(bf16), wd[...], preferred_element_type=f32)
            moe = moe + y * gates[:, col:col + 1]
        out_ref[...] = _layernorm(DEEPNORM_ALPHA * x + moe, g_ref[...], b_ref[...])

    @pl.when(j >= nused_ref[0])
    def _():
        out_ref[...] = jnp.zeros_like(out_ref)


def _experts(elo, ehi, nused, buf, wg, wu, wd, g, b):
    p = buf.shape[0]
    nb = p // MOE_BLK
    wspec = lambda shape, which: pl.BlockSpec(
        (None,) + shape, (lambda j, elo, ehi, nu: (elo[j], 0, 0)) if which == 0 else (lambda j, elo, ehi, nu: (ehi[j], 0, 0)))
    up_shape, down_shape = (D_MODEL, EXPERT_FF), (EXPERT_FF, D_MODEL)
    const = lambda shape: pl.BlockSpec(shape, lambda j, elo, ehi, nu: (0,) * len(shape))
    grid_spec = pltpu.PrefetchScalarGridSpec(
        num_scalar_prefetch=3,
        grid=(nb,),
        in_specs=[pl.BlockSpec((MOE_BLK, D_MODEL + SIDE), lambda j, elo, ehi, nu: (j, 0)),
                  wspec(up_shape, 0), wspec(up_shape, 0), wspec(down_shape, 0),
                  wspec(up_shape, 1), wspec(up_shape, 1), wspec(down_shape, 1),
                  const((1, D_MODEL)), const((1, D_MODEL))],
        out_specs=pl.BlockSpec((MOE_BLK, D_MODEL), lambda j, elo, ehi, nu: (j, 0)),
    )
    return pl.pallas_call(
        _expert_kernel,
        grid_spec=grid_spec,
        out_shape=jax.ShapeDtypeStruct((p, D_MODEL), f32),
        compiler_params=pltpu.CompilerParams(dimension_semantics=("arbitrary",), vmem_limit_bytes=VMEM_LIMIT),
        name="expert_pairs",
    )(elo, ehi, nused, buf, wg, wu, wd, wg, wu, wd, g, b)


def _rope_tables(seq):
    half = ROPE_DIM // 2
    inv_freq = jnp.power(jnp.float32(ROPE_THETA), -jnp.arange(half, dtype=f32) / half)
    ang = jnp.arange(seq, dtype=f32)[:, None] * inv_freq[None, :]
    lane = np.arange(LANES) % HEAD_DIM
    fidx = lane % half
    in_rope = lane < ROPE_DIM
    first = lane < half
    cos_l = jnp.cos(ang)[:, fidx]
    sin_l = jnp.sin(ang)[:, fidx]
    cos_t = jnp.where(in_rope[None, :], cos_l, 1.0)
    sa_t = jnp.where((in_rope & first)[None, :], -sin_l, 0.0)
    sb_t = jnp.where((in_rope & ~first)[None, :], sin_l, 0.0)
    return cos_t, sa_t, sb_t


_PAIRS = [(a, b) for a in range(EXPERTS_PER_GROUP) for b in range(a + 1, EXPERTS_PER_GROUP)]


def _dispatch_plan(counts, t):
    counts = counts.astype(jnp.int32)
    padded = ((counts + MOE_BLK - 1) // MOE_BLK) * MOE_BLK
    pend = jnp.cumsum(padded)
    pstart = pend - padded
    nb = (t + N_CLASSES * MOE_BLK) // MOE_BLK
    blk_start = jnp.arange(nb, dtype=jnp.int32) * MOE_BLK
    blk_cls = jnp.minimum(jnp.sum((blk_start[:, None] >= pend[None, :]).astype(jnp.int32), axis=1), N_CLASSES - 1)
    valid_end = pstart[blk_cls] + counts[blk_cls]
    zflag = ((blk_start + MOE_BLK > valid_end) | (blk_start >= pend[-1])).astype(jnp.int32)
    pair_lo = jnp.asarray([p[0] for p in _PAIRS], jnp.int32)
    pair_hi = jnp.asarray([p[1] for p in _PAIRS], jnp.int32)
    grp = blk_cls // N_PAIRS
    elo = grp * EXPERTS_PER_GROUP + pair_lo[blk_cls % N_PAIRS]
    ehi = grp * EXPERTS_PER_GROUP + pair_hi[blk_cls % N_PAIRS]
    nused = (pend[-1] // MOE_BLK).astype(jnp.int32).reshape(1)
    pstart_pad = jnp.zeros((CLASS_ROWS,), jnp.int32).at[:N_CLASSES].set(pstart)
    return pstart_pad, zflag, elo, ehi, nused


def kernel(x, w_in, w_out, gmlp_ln_g, gmlp_ln_b, gmlp_w_s, gmlp_b_s, pool_w, pool_scale, ln1_g, ln1_b,
           router_w, router_bias, w_gate, w_up, w_down, ln2_g, ln2_b):
    batch, seq, d = x.shape
    t = batch * seq
    assert d == D_MODEL and seq % ROW_TILE == 0
    cos_t, sa_t, sb_t = _rope_tables(seq)
    tri = (np.arange(ROW_TILE)[:, None] < np.arange(ROW_TILE)[None, :])
    tri = jnp.asarray(tri, bf16)
    rw_pad = router_w.T.astype(bf16)
    rb = router_bias.reshape(N_EXPERTS, 1)
    n_rows = t + N_CLASSES * MOE_BLK
    dummy_flag = jnp.zeros((1,), jnp.int32)

    xf = x.reshape(t, d)
    for l in range(DEPTH):
        lng = gmlp_ln_g[l].reshape(1, GMLP_WIDTH)
        lnb = gmlp_ln_b[l].reshape(1, GMLP_WIDTH)
        qp, kp, vp, u, vln, pz = _inproj(xf, w_in[l].astype(bf16), cos_t, sa_t, sb_t, lng, lnb, seq)
        attn = _attention(qp, kp, vp, batch, seq)
        bs_full = jnp.repeat(gmlp_b_s[l].T, HEAD_DIM, axis=1)
        wp_bd = jnp.zeros((POOL_WIDTH, POOL_WIDTH), f32)
        for g in range(len(POOL_WINDOWS)):
            wp_bd = wp_bd.at[g * HEAD_DIM:(g + 1) * HEAD_DIM, g * HEAD_DIM:(g + 1) * HEAD_DIM].set(pool_w[l, g])
        wp_bd = wp_bd.astype(bf16)
        gm, pool = _mixer(u, vln, pz, gmlp_w_s[l].astype(bf16), bs_full, wp_bd,
                          pool_scale[l].reshape(1, POOL_WIDTH), seq)
        xe, meta, counts = _outproj(attn, gm, pool, xf, w_out[l].astype(bf16), ln1_g[l].reshape(1, d),
                                    ln1_b[l].reshape(1, d), rw_pad, rb, tri)
        pstart, zflag, elo, ehi, nused = _dispatch_plan(counts[:N_CLASSES, 0], t)
        buf = _permute(meta, pstart, zflag, xe, n_rows, D_MODEL + SIDE, scatter=True)
        ys = _experts(elo, ehi, nused, buf, w_gate[l].astype(bf16), w_up[l].astype(bf16),
                      w_down[l].astype(bf16), ln2_g[l].reshape(1, d), ln2_b[l].reshape(1, d))
        xf = _permute(meta, pstart, dummy_flag, ys, t, D_MODEL, scatter=False)
    return xf.reshape(batch, seq, d)
```

```python
import functools

import jax
import jax.numpy as jnp
import numpy as np
from jax import lax
from jax.experimental import pallas as pl
from jax.experimental.pallas import tpu as pltpu

f32 = jnp.float32
bf16 = jnp.bfloat16

D_MODEL = 1024
HEAD_DIM = 64
ATTN_WIDTH = 512
DILATED_CONFIGS = ((128, 1), (512, 4), (2048, 16))
ATTN_HALF = 64
ROPE_THETA = 500000.0
ROPE_DIM = 16
GMLP_WIDTH = 256
GMLP_GROUPS = 4
GMLP_CHUNK = 128
POOL_WIDTH = 256
POOL_WINDOWS = (2, 4, 8, 16)
IN_COLS = 3 * ATTN_WIDTH + 2 * GMLP_WIDTH + POOL_WIDTH
N_EXPERTS = 16
N_EXPERT_GROUPS = 4
EXPERTS_PER_GROUP = 4
EXPERT_FF = 512
DEPTH = 2
DEEPNORM_ALPHA = float((2 * DEPTH) ** 0.25)
LN_EPS = 1e-5
NEG_INF = -1e30

LANES = 128
ROW_TILE = 512
ATTN_QBLK = 128
POOL_HALO = 32
N_PAIRS = 6
N_CLASSES = N_EXPERT_GROUPS * N_PAIRS
CLASS_ROWS = 32
MOE_BLK = 256
SIDE = LANES
PERM_CHUNK = ROW_TILE
PERM_UNROLL = 8
VMEM_LIMIT = 48 * 1024 * 1024


def _layernorm(y, g, b):
    mu = jnp.mean(y, axis=-1, keepdims=True)
    d = y - mu
    var = jnp.mean(d * d, axis=-1, keepdims=True)
    return d * lax.rsqrt(var + LN_EPS) * g + b


def _group_mean(v):
    lane = lax.broadcasted_iota(jnp.int32, v.shape, 1)
    out = jnp.zeros_like(v)
    for g in range(GMLP_GROUPS):
        m = (lane >= g * HEAD_DIM) & (lane < (g + 1) * HEAD_DIM)
        s = jnp.sum(jnp.where(m, v, 0.0), axis=-1, keepdims=True)
        out = jnp.where(m, s, out)
    return out * (1.0 / HEAD_DIM)


def _inproj_kernel(x_ref, w_ref, cos_ref, sa_ref, sb_ref, lng_ref, lnb_ref,
                   q_ref, k_ref, v_ref, u_ref, vln_ref, pz_ref):
    xb = x_ref[...].astype(bf16)
    cos = cos_ref[...]
    sa = sa_ref[...]
    sb = sb_ref[...]
    for base, out_ref, rope, scale in ((0, q_ref, True, HEAD_DIM ** -0.5),
                                       (ATTN_WIDTH, k_ref, True, 1.0),
                                       (2 * ATTN_WIDTH, v_ref, False, 1.0)):
        y = jnp.dot(xb, w_ref[:, base:base + ATTN_WIDTH], preferred_element_type=f32)
        for p in range(ATTN_WIDTH // LANES):
            yp = y[:, p * LANES:(p + 1) * LANES]
            if rope:
                yp = yp * cos + pltpu.roll(yp, LANES - ROPE_DIM // 2, 1) * sa + pltpu.roll(yp, ROPE_DIM // 2, 1) * sb
            if scale != 1.0:
                yp = yp * scale
            out_ref[p] = yp.astype(bf16)
    o3 = 3 * ATTN_WIDTH
    y = jnp.dot(xb, w_ref[:, o3:o3 + 3 * GMLP_WIDTH], preferred_element_type=f32)
    u_ref[...] = jax.nn.gelu(y[:, :GMLP_WIDTH])
    gv = jax.nn.gelu(y[:, GMLP_WIDTH:2 * GMLP_WIDTH])
    mu = _group_mean(gv)
    d = gv - mu
    var = _group_mean(d * d)
    vln_ref[...] = (d * lax.rsqrt(var + LN_EPS) * lng_ref[...] + lnb_ref[...]).astype(bf16)
    pz_ref[...] = y[:, 2 * GMLP_WIDTH:]


def _inproj(x, w_in, cos_t, sa_t, sb_t, lng, lnb, seq):
    t = x.shape[0]
    nt = t // ROW_TILE
    tiles_per_seq = seq // ROW_TILE
    npair = ATTN_WIDTH // LANES
    tab_spec = pl.BlockSpec((ROW_TILE, LANES), lambda i: (i % tiles_per_seq, 0))
    row = lambda w: pl.BlockSpec((ROW_TILE, w), lambda i: (i, 0))
    const = lambda shape: pl.BlockSpec(shape, lambda i: (0,) * len(shape))
    pair_spec = pl.BlockSpec((npair, ROW_TILE, LANES), lambda i: (0, i, 0))
    pair_shape = jax.ShapeDtypeStruct((npair, t, LANES), bf16)
    return pl.pallas_call(
        _inproj_kernel,
        grid=(nt,),
        in_specs=[row(D_MODEL), const((D_MODEL, IN_COLS)), tab_spec, tab_spec, tab_spec,
                  const((1, GMLP_WIDTH)), const((1, GMLP_WIDTH))],
        out_specs=[pair_spec, pair_spec, pair_spec, row(GMLP_WIDTH), row(GMLP_WIDTH), row(POOL_WIDTH)],
        out_shape=[pair_shape, pair_shape, pair_shape,
                   jax.ShapeDtypeStruct((t, GMLP_WIDTH), f32),
                   jax.ShapeDtypeStruct((t, GMLP_WIDTH), bf16),
                   jax.ShapeDtypeStruct((t, POOL_WIDTH), f32)],
        compiler_params=pltpu.CompilerParams(dimension_semantics=("parallel",), vmem_limit_bytes=VMEM_LIMIT),
        name="inproj",
    )(x, w_in, cos_t, sa_t, sb_t, lng, lnb)


def _attn_kernel(q_ref, k_ref, v_ref, bias_ref, o_ref, qs, ks, vs, o0, o1, o2, l0s, l1s, l2s, *, seq):
    os_ = (o0, o1, o2)
    ls_ = (l0s, l1s, l2s)
    qs[...] = q_ref[...].astype(f32)
    ks[...] = k_ref[...].astype(f32)
    vs[...] = v_ref[...].astype(f32)
    lane = lax.broadcasted_iota(jnp.int32, (ATTN_QBLK, LANES), 1)
    head_a = lane < HEAD_DIM

    for ci, (_, dil) in enumerate(DILATED_CONFIGS):
        length = seq // dil
        nblk = length // ATTN_QBLK
        win = min(2 * ATTN_QBLK, length)
        shift = dil.bit_length() - 1

        def rows(start, size, dil=dil):
            if dil == 1:
                return pl.ds(pl.multiple_of(start, 8), size)
            return pl.ds(start, size, stride=dil)

        def body(u, carry, ci=ci, dil=dil, length=length, win=win, shift=shift, rows=rows):
            c = u & (dil - 1)
            i = u >> shift
            k0 = jnp.clip(i * ATTN_QBLK - ATTN_HALF, 0, length - win)
            delta = k0 - i * ATTN_QBLK
            qrows = rows(c + dil * ATTN_QBLK * i, ATTN_QBLK)
            krows = rows(c + dil * k0, win)
            q = qs[qrows, :]
            kb = ks[krows, :].astype(bf16)
            vb = vs[krows, :].astype(bf16)
            bias = bias_ref[lax.shift_right_logical(-delta, ATTN_HALF.bit_length() - 1), :, 0:win]
            q2 = jnp.concatenate([jnp.where(head_a, q, 0.0), jnp.where(head_a, 0.0, q)], axis=0).astype(bf16)
            s = lax.dot_general(q2, kb, (((1,), (1,)), ((), ())), preferred_element_type=f32) + bias
            m = jnp.max(s, axis=-1, keepdims=True)
            p = jnp.exp(s - m).astype(bf16)
            ol = jnp.dot(p, jnp.concatenate([vb, jnp.ones_like(vb)], axis=1), preferred_element_type=f32)
            l = ol[:, LANES:]
            o = ol[:, :LANES] / l
            lse = m + jnp.log(l)
            os_[ci][qrows, :] = jnp.where(head_a, o[:ATTN_QBLK], o[ATTN_QBLK:])
            ls_[ci][qrows, :] = jnp.where(head_a, lse[:ATTN_QBLK], lse[ATTN_QBLK:])
            return carry

        lax.fori_loop(0, dil * nblk, body, 0, unroll=8)

    chunk = 256

    def combine(j, carry):
        r = pl.ds(pl.multiple_of(j * chunk, chunk), chunk)
        l0, l1, l2 = l0s[r, :], l1s[r, :], l2s[r, :]
        mx = jnp.maximum(jnp.maximum(l0, l1), l2)
        w0, w1, w2 = jnp.exp(l0 - mx), jnp.exp(l1 - mx), jnp.exp(l2 - mx)
        num = w0 * o0[r, :] + w1 * o1[r, :] + w2 * o2[r, :]
        o_ref[r, :] = (num / (w0 + w1 + w2)).astype(bf16)
        return carry

    lax.fori_loop(0, seq // chunk, combine, 0)


def _attention(qp, kp, vp, batch, seq):
    npair = qp.shape[0]
    t = batch * seq
    in_spec = pl.BlockSpec((None, seq, LANES), lambda p, b: (p, b, 0))
    win = 2 * ATTN_QBLK
    rr = np.arange(2 * ATTN_QBLK)[None, :, None] % ATTN_QBLK
    cc = np.arange(win)[None, None, :]
    dd = np.arange(3)[:, None, None] * ATTN_HALF
    bias = jnp.asarray(np.where(np.abs(cc - dd - rr) <= ATTN_HALF, 0.0, NEG_INF), f32)
    return pl.pallas_call(
        functools.partial(_attn_kernel, seq=seq),
        grid=(npair, batch),
        in_specs=[in_spec, in_spec, in_spec, pl.BlockSpec(bias.shape, lambda p, b: (0, 0, 0))],
        out_specs=pl.BlockSpec((seq, LANES), lambda p, b: (b, p)),
        out_shape=jax.ShapeDtypeStruct((t, ATTN_WIDTH), bf16),
        scratch_shapes=[pltpu.VMEM((seq, LANES), f32)] * (3 + 2 * len(DILATED_CONFIGS)),
        compiler_params=pltpu.CompilerParams(dimension_semantics=("parallel", "parallel"),
                                             vmem_limit_bytes=VMEM_LIMIT),
        name="dilated_attn",
    )(qp, kp, vp, bias)


def _mixer_kernel(u_ref, vln_ref, pz_ref, prev_ref, next_ref, ws_ref, bs_ref, wp_ref, sc_ref,
                  gm_ref, pool_ref, z0, z1, z2, z3, z4, *, seq):
    tm = ROW_TILE
    h = POOL_HALO
    lane = lax.broadcasted_iota(jnp.int32, (GMLP_CHUNK, GMLP_WIDTH), 1)
    for cc in range(tm // GMLP_CHUNK):
        r = slice(cc * GMLP_CHUNK, (cc + 1) * GMLP_CHUNK)
        vc = vln_ref[r, :]
        sv = jnp.zeros((GMLP_CHUNK, GMLP_WIDTH), f32)
        for g in range(GMLP_GROUPS):
            svg = jnp.dot(ws_ref[g], vc, preferred_element_type=f32)
            sv = jnp.where((lane >= g * HEAD_DIM) & (lane < (g + 1) * HEAD_DIM), svg, sv)
        gm_ref[r, :] = (u_ref[r, :] * (sv + bs_ref[...])).astype(bf16)

    i = pl.program_id(0)
    tiles_per_seq = seq // tm
    pos0 = (i % tiles_per_seq) * tm
    first = pos0 == 0
    last = pos0 + tm == seq
    z0[0:h, :] = jnp.where(first, 0.0, prev_ref[...])
    z0[h:h + tm, :] = pz_ref[...]
    z0[h + tm:h + tm + h, :] = jnp.where(last, 0.0, next_ref[...])
    z1[8:tm + 2 * h - 8, :] = z0[8:tm + 2 * h - 8, :] + z0[7:tm + 2 * h - 9, :]
    z2[16:tm + 2 * h - 16, :] = z1[15:tm + 2 * h - 17, :] + z1[17:tm + 2 * h - 15, :]
    z3[24:tm + 2 * h - 24, :] = z2[22:tm + 2 * h - 26, :] + z2[26:tm + 2 * h - 22, :]
    z4[h:h + tm, :] = z3[h - 4:h + tm - 4, :] + z3[h + 4:h + tm + 4, :]
    lane_t = lax.broadcasted_iota(jnp.int32, (tm, POOL_WIDTH), 1)
    pos = pos0 + lax.broadcasted_iota(jnp.int32, (tm, POOL_WIDTH), 0)
    zc = z0[h:h + tm, :]
    pooled = jnp.zeros((tm, POOL_WIDTH), f32)
    for g, (w, zw) in enumerate(zip(POOL_WINDOWS, (z1, z2, z3, z4))):
        left = w // 2
        right = w - 1 - left
        lo = jnp.maximum(pos - left, 0)
        hi = jnp.minimum(pos + right + 1, seq)
        cnt = (hi - lo).astype(f32)
        val = zw[h:h + tm, :] / cnt - zc
        pooled = jnp.where((lane_t >= g * HEAD_DIM) & (lane_t < (g + 1) * HEAD_DIM), val, pooled)
    y = jnp.dot(pooled.astype(bf16), wp_ref[...], preferred_element_type=f32)
    pool_ref[...] = (y * sc_ref[...]).astype(bf16)


def _mixer(u, vln, pz, ws, bs_full, wp_bd, scale, seq):
    t = u.shape[0]
    tm, h = ROW_TILE, POOL_HALO
    nt = t // tm
    hb = tm // h
    row = lambda w: pl.BlockSpec((tm, w), lambda i: (i, 0))
    const = lambda shape: pl.BlockSpec(shape, lambda i: (0,) * len(shape))
    prev_spec = pl.BlockSpec((h, POOL_WIDTH), lambda i: (jnp.maximum(i * hb - 1, 0), 0))
    next_spec = pl.BlockSpec((h, POOL_WIDTH), lambda i: (jnp.minimum((i + 1) * hb, t // h - 1), 0))
    zbuf = pltpu.VMEM((tm + 2 * h, POOL_WIDTH), f32)
    return pl.pallas_call(
        functools.partial(_mixer_kernel, seq=seq),
        grid=(nt,),
        in_specs=[row(GMLP_WIDTH), row(GMLP_WIDTH), row(POOL_WIDTH), prev_spec, next_spec,
                  const((GMLP_GROUPS, GMLP_CHUNK, GMLP_CHUNK)), const((GMLP_CHUNK, GMLP_WIDTH)),
                  const((POOL_WIDTH, POOL_WIDTH)), const((1, POOL_WIDTH))],
        out_specs=[row(GMLP_WIDTH), row(POOL_WIDTH)],
        out_shape=[jax.ShapeDtypeStruct((t, GMLP_WIDTH), bf16), jax.ShapeDtypeStruct((t, POOL_WIDTH), bf16)],
        scratch_shapes=[zbuf] * 5,
        compiler_params=pltpu.CompilerParams(dimension_semantics=("parallel",), vmem_limit_bytes=VMEM_LIMIT),
        name="mixer",
    )(u, vln, pz, pz, pz, ws, bs_full, wp_bd, scale)


def _top2_of4(b, s):
    v1, i1, s1 = b[0], jnp.zeros(b[0].shape, jnp.int32), s[0]
    for j in range(1, 4):
        gt = b[j] > v1
        v1 = jnp.where(gt, b[j], v1)
        i1 = jnp.where(gt, j, i1)
        s1 = jnp.where(gt, s[j], s1)
    v2 = jnp.full(b[0].shape, -jnp.inf, f32)
    i2 = jnp.full(b[0].shape, -1, jnp.int32)
    s2 = jnp.zeros(b[0].shape, f32)
    for j in range(4):
        cand = jnp.where(i1 != j, b[j], -jnp.inf) > v2
        v2 = jnp.where(cand, b[j], v2)
        i2 = jnp.where(cand, j, i2)
        s2 = jnp.where(cand, s[j], s2)
    return v1 + v2, i1, i2, s1, s2


def _outproj_kernel(attn_ref, gm_ref, pool_ref, x_ref, w_ref, g_ref, b_ref, rw_ref, rb_ref, tri_ref,
                    xe_ref, meta_ref, cnt_ref, carry):
    tm = ROW_TILE

    @pl.when(pl.program_id(0) == 0)
    def _():
        carry[...] = jnp.zeros_like(carry)

    o1, o2 = ATTN_WIDTH, ATTN_WIDTH + GMLP_WIDTH
    hmix = jnp.dot(attn_ref[...], w_ref[0:o1, :], preferred_element_type=f32)
    hmix = hmix + jnp.dot(gm_ref[...], w_ref[o1:o2, :], preferred_element_type=f32)
    hmix = hmix + jnp.dot(pool_ref[...], w_ref[o2:, :], preferred_element_type=f32)
    x1 = _layernorm(DEEPNORM_ALPHA * x_ref[...] + hmix, g_ref[...], b_ref[...])
    xe_ref[:, 0:D_MODEL] = x1

    logits = lax.dot_general(rw_ref[...], x1.astype(bf16), (((1,), (1,)), ((), ())),
                             preferred_element_type=f32)
    scores = 1.0 / (1.0 + jnp.exp(-logits))
    biased = scores + rb_ref[...]
    brow = [biased[e:e + 1, :] for e in range(N_EXPERTS)]
    srow = [scores[e:e + 1, :] for e in range(N_EXPERTS)]
    best = None
    for g in range(N_EXPERT_GROUPS):
        sl = slice(g * EXPERTS_PER_GROUP, (g + 1) * EXPERTS_PER_GROUP)
        gs, i1, i2, s1, s2 = _top2_of4(brow[sl], srow[sl])
        if best is None:
            best = (gs, jnp.zeros(gs.shape, jnp.int32), i1, i2, s1, s2)
        else:
            gt = gs > best[0]
            best = (jnp.where(gt, gs, best[0]), jnp.where(gt, g, best[1]), jnp.where(gt, i1, best[2]),
                    jnp.where(gt, i2, best[3]), jnp.where(gt, s1, best[4]), jnp.where(gt, s2, best[5]))
    _, gsel, i1, i2, s1, s2 = best
    den = s1 + s2
    ga, gb = s1 / den, s2 / den
    first_lo = i1 < i2
    lo = jnp.minimum(i1, i2)
    hi = jnp.maximum(i1, i2)
    g_lo = jnp.where(first_lo, ga, gb)
    g_hi = jnp.where(first_lo, gb, ga)
    pidx = jnp.where(lo == 0, hi - 1, jnp.where(lo == 1, hi + 1, N_PAIRS - 1))
    cls = gsel * N_PAIRS + pidx

    onehot = lax.broadcasted_iota(jnp.int32, (CLASS_ROWS, tm), 0) == cls
    ohf = jnp.where(onehot, 1.0, 0.0)
    prefix = jnp.dot(ohf.astype(bf16), tri_ref[...], preferred_element_type=f32)
    base = carry[:, 0:1]
    rank = jnp.sum(ohf * (prefix + base), axis=0, keepdims=True)
    new_carry = carry[...] + jnp.sum(ohf, axis=1, keepdims=True)
    carry[...] = new_carry
    cnt_ref[...] = new_carry

    srow_id = lax.broadcasted_iota(jnp.int32, (8, tm), 0)
    meta_ref[...] = jnp.where(srow_id == 0, cls, jnp.where(srow_id == 1, rank.astype(jnp.int32), 0))
    side8 = jnp.where(srow_id == 0, g_lo, jnp.where(srow_id == 1, g_hi, 0.0))
    side = jnp.concatenate([side8, jnp.zeros((SIDE - 8, tm), f32)], axis=0)
    xe_ref[:, D_MODEL:] = side.T


def _outproj(attn, gm, pool, x, w_out, g, b, rw_pad, rb, tri):
    t = x.shape[0]
    tm = ROW_TILE
    row = lambda w: pl.BlockSpec((tm, w), lambda i: (i, 0))
    const = lambda shape: pl.BlockSpec(shape, lambda i: (0,) * len(shape))
    return pl.pallas_call(
        _outproj_kernel,
        grid=(t // tm,),
        in_specs=[row(ATTN_WIDTH), row(GMLP_WIDTH), row(POOL_WIDTH), row(D_MODEL), const((D_MODEL, D_MODEL)),
                  const((1, D_MODEL)), const((1, D_MODEL)), const((N_EXPERTS, D_MODEL)), const((N_EXPERTS, 1)),
                  const((tm, tm))],
        out_specs=[row(D_MODEL + SIDE), pl.BlockSpec((None, 8, tm), lambda i: (i, 0, 0)),
                   const((CLASS_ROWS, LANES))],
        out_shape=[jax.ShapeDtypeStruct((t, D_MODEL + SIDE), f32), jax.ShapeDtypeStruct((t // tm, 8, tm), jnp.int32),
                   jax.ShapeDtypeStruct((CLASS_ROWS, LANES), f32)],
        scratch_shapes=[pltpu.VMEM((CLASS_ROWS, LANES), f32)],
        compiler_params=pltpu.CompilerParams(dimension_semantics=("arbitrary",), vmem_limit_bytes=VMEM_LIMIT),
        name="outproj_ln_router",
    )(attn, gm, pool, x, w_out, g, b, rw_pad, rb, tri)


def _permute_kernel(idx_ref, zflag_ref, src_ref, *rest, scatter, nzero):
    if scatter:
        dst_ref, zeros, sem, zsem = rest
    else:
        dst_ref, sem = rest
    step = pl.program_id(0)

    if scatter:
        def zcopy(j):
            return pltpu.make_async_copy(zeros, dst_ref.at[pl.ds(pl.multiple_of(j * MOE_BLK, MOE_BLK), MOE_BLK), :],
                                         zsem)

        @pl.when(step == 0)
        def _():
            zeros[...] = jnp.zeros_like(zeros)

            def zstart(j, c):
                @pl.when(zflag_ref[j] != 0)
                def _():
                    zcopy(j).start()
                return c

            def zwait(j, c):
                @pl.when(zflag_ref[j] != 0)
                def _():
                    zcopy(j).wait()
                return c

            lax.fori_loop(0, nzero, zstart, 0)
            lax.fori_loop(0, nzero, zwait, 0)

    def row_copy(r):
        other = idx_ref[0, 0, r]
        if scatter:
            return pltpu.make_async_copy(src_ref.at[pl.ds(r, 1), :], dst_ref.at[pl.ds(other, 1), :], sem)
        return pltpu.make_async_copy(src_ref.at[pl.ds(other, 1), :], dst_ref.at[pl.ds(r, 1), :], sem)

    def start(g, c):
        for k in range(PERM_UNROLL):
            row_copy(g * PERM_UNROLL + k).start(priority=k % 2)
        return c

    lax.fori_loop(0, PERM_CHUNK // PERM_UNROLL, start, 0)
    if scatter:
        pltpu.make_async_copy(src_ref, dst_ref.at[pl.ds(0, PERM_CHUNK), :], sem).wait()
    else:
        pltpu.make_async_copy(src_ref.at[pl.ds(0, PERM_CHUNK), :], dst_ref, sem).wait()


def _permute(idx, zflag, src, n_dst, width, scatter):
    nsteps = idx.shape[0]
    nzero = zflag.shape[0]
    scratch = [pltpu.SemaphoreType.DMA(())]
    if scatter:
        scratch = [pltpu.VMEM((MOE_BLK, width), f32), pltpu.SemaphoreType.DMA(()), pltpu.SemaphoreType.DMA(())]
    block = pl.BlockSpec((PERM_CHUNK, width), lambda i: (i, 0))
    hbm = pl.BlockSpec(memory_space=pl.ANY)
    smem = pl.BlockSpec(memory_space=pltpu.SMEM)
    return pl.pallas_call(
        functools.partial(_permute_kernel, scatter=scatter, nzero=nzero),
        grid=(nsteps,),
        in_specs=[pl.BlockSpec((1, 1, PERM_CHUNK), lambda i: (i, 0, 0), memory_space=pltpu.SMEM), smem,
                  block if scatter else hbm],
        out_specs=hbm if scatter else block,
        out_shape=jax.ShapeDtypeStruct((n_dst, width), f32),
        scratch_shapes=scratch,
        compiler_params=pltpu.CompilerParams(dimension_semantics=("arbitrary",), vmem_limit_bytes=VMEM_LIMIT),
        name="dispatch_rows" if scatter else "undispatch_rows",
    )(idx, zflag, src)


def _expert_kernel(elo_ref, ehi_ref, nused_ref, new_ref, xb_ref, wg_lo, wu_lo, wd_lo, wg_hi, wu_hi, wd_hi,
                   g_ref, b_ref, out_ref, *wb):
    j = pl.program_id(0)

    @pl.when(new_ref[j] != 0)
    def _():
        for src, dst in zip((wg_lo, wu_lo, wd_lo, wg_hi, wu_hi, wd_hi), wb):
            dst[...] = src[...].astype(bf16)

    @pl.when(j < nused_ref[0])
    def _():
        x = xb_ref[:, 0:D_MODEL]
        gates = xb_ref[:, D_MODEL:D_MODEL + SIDE]
        xb = x.astype(bf16)
        moe = jnp.zeros_like(x)
        for col, (wg, wu, wd) in enumerate((wb[0:3], wb[3:6])):
            a = jnp.dot(xb, wg[...], preferred_element_type=f32)
            up = jnp.dot(xb, wu[...], preferred_element_type=f32)
            hidden = (a / (1.0 + jnp.exp(-a))) * up
            y = jnp.dot(hidden.astype(bf16), wd[...], preferred_element_type=f32)
            moe = moe + y * gates[:, col:col + 1]
        out_ref[...] = _layernorm(DEEPNORM_ALPHA * x + moe, g_ref[...], b_ref[...])

    @pl.when(j >= nused_ref[0])
    def _():
        out_ref[...] = jnp.zeros_like(out_ref)


def _experts(elo, ehi, nused, newcls, buf, wg, wu, wd, g, b):
    p = buf.shape[0]
    nb = p // MOE_BLK
    wspec = lambda shape, which: pl.BlockSpec(
        (None,) + shape, (lambda j, elo, ehi, nu, nw: (elo[j], 0, 0)) if which == 0
        else (lambda j, elo, ehi, nu, nw: (ehi[j], 0, 0)))
    up_shape, down_shape = (D_MODEL, EXPERT_FF), (EXPERT_FF, D_MODEL)
    const = lambda shape: pl.BlockSpec(shape, lambda j, elo, ehi, nu, nw: (0,) * len(shape))
    grid_spec = pltpu.PrefetchScalarGridSpec(
        num_scalar_prefetch=4,
        grid=(nb,),
        in_specs=[pl.BlockSpec((MOE_BLK, D_MODEL + SIDE), lambda j, elo, ehi, nu, nw: (j, 0)),
                  wspec(up_shape, 0), wspec(up_shape, 0), wspec(down_shape, 0),
                  wspec(up_shape, 1), wspec(up_shape, 1), wspec(down_shape, 1),
                  const((1, D_MODEL)), const((1, D_MODEL))],
        out_specs=pl.BlockSpec((MOE_BLK, D_MODEL), lambda j, elo, ehi, nu, nw: (j, 0)),
        scratch_shapes=[pltpu.VMEM(up_shape, bf16), pltpu.VMEM(up_shape, bf16), pltpu.VMEM(down_shape, bf16)] * 2,
    )
    return pl.pallas_call(
        _expert_kernel,
        grid_spec=grid_spec,
        out_shape=jax.ShapeDtypeStruct((p, D_MODEL), f32),
        compiler_params=pltpu.CompilerParams(dimension_semantics=("arbitrary",), vmem_limit_bytes=VMEM_LIMIT),
        name="expert_pairs",
    )(elo, ehi, nused, newcls, buf, wg, wu, wd, wg, wu, wd, g, b)


def _rope_tables(seq):
    half = ROPE_DIM // 2
    inv_freq = jnp.power(jnp.float32(ROPE_THETA), -jnp.arange(half, dtype=f32) / half)
    ang = jnp.arange(seq, dtype=f32)[:, None] * inv_freq[None, :]
    lane = np.arange(LANES) % HEAD_DIM
    fidx = lane % half
    in_rope = lane < ROPE_DIM
    first = lane < half
    cos_l = jnp.cos(ang)[:, fidx]
    sin_l = jnp.sin(ang)[:, fidx]
    cos_t = jnp.where(in_rope[None, :], cos_l, 1.0)
    sa_t = jnp.where((in_rope & first)[None, :], -sin_l, 0.0)
    sb_t = jnp.where((in_rope & ~first)[None, :], sin_l, 0.0)
    return cos_t, sa_t, sb_t


_PAIRS = [(a, b) for a in range(EXPERTS_PER_GROUP) for b in range(a + 1, EXPERTS_PER_GROUP)]


def _dispatch_plan(counts, meta, t):
    counts = counts.astype(jnp.int32)
    padded = ((counts + MOE_BLK - 1) // MOE_BLK) * MOE_BLK
    pend = jnp.cumsum(padded)
    pstart = pend - padded
    nb = (t + N_CLASSES * MOE_BLK) // MOE_BLK
    blk_start = jnp.arange(nb, dtype=jnp.int32) * MOE_BLK
    blk_cls = jnp.minimum(jnp.sum((blk_start[:, None] >= pend[None, :]).astype(jnp.int32), axis=1), N_CLASSES - 1)
    valid_end = pstart[blk_cls] + counts[blk_cls]
    zflag = ((blk_start + MOE_BLK > valid_end) | (blk_start >= pend[-1])).astype(jnp.int32)
    pair_lo = jnp.asarray([p[0] for p in _PAIRS], jnp.int32)
    pair_hi = jnp.asarray([p[1] for p in _PAIRS], jnp.int32)
    grp = blk_cls // N_PAIRS
    elo = grp * EXPERTS_PER_GROUP + pair_lo[blk_cls % N_PAIRS]
    ehi = grp * EXPERTS_PER_GROUP + pair_hi[blk_cls % N_PAIRS]
    nused = (pend[-1] // MOE_BLK).astype(jnp.int32).reshape(1)
    newcls = jnp.concatenate([jnp.ones((1,), jnp.int32), (blk_cls[1:] != blk_cls[:-1]).astype(jnp.int32)])
    cls, rank = meta[:, 0:1, :], meta[:, 1:2, :]
    dest = rank
    for c in range(N_CLASSES):
        dest = dest + jnp.where(cls == c, pstart[c], 0)
    return dest, zflag, elo, ehi, nused, newcls


def kernel(x, w_in, w_out, gmlp_ln_g, gmlp_ln_b, gmlp_w_s, gmlp_b_s, pool_w, pool_scale, ln1_g, ln1_b,
           router_w, router_bias, w_gate, w_up, w_down, ln2_g, ln2_b):
    batch, seq, d = x.shape
    t = batch * seq
    assert d == D_MODEL and seq % ROW_TILE == 0
    cos_t, sa_t, sb_t = _rope_tables(seq)
    tri = (np.arange(ROW_TILE)[:, None] < np.arange(ROW_TILE)[None, :])
    tri = jnp.asarray(tri, bf16)
    rw_pad = router_w.T.astype(bf16)
    rb = router_bias.reshape(N_EXPERTS, 1)
    n_rows = t + N_CLASSES * MOE_BLK
    dummy_flag = jnp.zeros((1,), jnp.int32)

    xf = x.reshape(t, d)
    for l in range(DEPTH):
        lng = gmlp_ln_g[l].reshape(1, GMLP_WIDTH)
        lnb = gmlp_ln_b[l].reshape(1, GMLP_WIDTH)
        qp, kp, vp, u, vln, pz = _inproj(xf, w_in[l].astype(bf16), cos_t, sa_t, sb_t, lng, lnb, seq)
        attn = _attention(qp, kp, vp, batch, seq)
        bs_full = jnp.repeat(gmlp_b_s[l].T, HEAD_DIM, axis=1)
        wp_bd = jnp.zeros((POOL_WIDTH, POOL_WIDTH), f32)
        for g in range(len(POOL_WINDOWS)):
            wp_bd = wp_bd.at[g * HEAD_DIM:(g + 1) * HEAD_DIM, g * HEAD_DIM:(g + 1) * HEAD_DIM].set(pool_w[l, g])
        wp_bd = wp_bd.astype(bf16)
        gm, pool = _mixer(u, vln, pz, gmlp_w_s[l].astype(bf16), bs_full, wp_bd,
                          pool_scale[l].reshape(1, POOL_WIDTH), seq)
        xe, meta, counts = _outproj(attn, gm, pool, xf, w_out[l].astype(bf16), ln1_g[l].reshape(1, d),
                                    ln1_b[l].reshape(1, d), rw_pad, rb, tri)
        dest, zflag, elo, ehi, nused, newcls = _dispatch_plan(counts[:N_CLASSES, 0], meta, t)
        buf = _permute(dest, zflag, xe, n_rows, D_MODEL + SIDE, scatter=True)
        ys = _experts(elo, ehi, nused, newcls, buf, w_gate[l], w_up[l], w_down[l],
                      ln2_g[l].reshape(1, d), ln2_b[l].reshape(1, d))
        xf = _permute(dest, dummy_flag, ys, t, D_MODEL, scatter=False)
    return xf.reshape(batch, seq, d)
```

```python
import functools

import jax
import jax.numpy as jnp
import numpy as np
from jax import lax
from jax.experimental import pallas as pl
from jax.experimental.pallas import tpu as pltpu

f32 = jnp.float32
bf16 = jnp.bfloat16

D_MODEL = 1024
HEAD_DIM = 64
ATTN_WIDTH = 512
DILATED_CONFIGS = ((128, 1), (512, 4), (2048, 16))
ATTN_HALF = 64
ROPE_THETA = 500000.0
ROPE_DIM = 16
GMLP_WIDTH = 256
GMLP_GROUPS = 4
GMLP_CHUNK = 128
POOL_WIDTH = 256
POOL_WINDOWS = (2, 4, 8, 16)
IN_COLS = 3 * ATTN_WIDTH + 2 * GMLP_WIDTH + POOL_WIDTH
N_EXPERTS = 16
N_EXPERT_GROUPS = 4
EXPERTS_PER_GROUP = 4
EXPERT_FF = 512
DEPTH = 2
DEEPNORM_ALPHA = float((2 * DEPTH) ** 0.25)
LN_EPS = 1e-5
NEG_INF = -1e30

LANES = 128
ROW_TILE = 512
ATTN_QBLK = 128
POOL_HALO = 32
N_PAIRS = 6
N_CLASSES = N_EXPERT_GROUPS * N_PAIRS
CLASS_ROWS = 32
MOE_BLK = 256
SIDE = LANES
PERM_CHUNK = 1024
PERM_UNROLL = 16
VMEM_LIMIT = 48 * 1024 * 1024


def _layernorm(y, g, b):
    mu = jnp.mean(y, axis=-1, keepdims=True)
    d = y - mu
    var = jnp.mean(d * d, axis=-1, keepdims=True)
    return d * lax.rsqrt(var + LN_EPS) * g + b


def _group_mean(v):
    lane = lax.broadcasted_iota(jnp.int32, v.shape, 1)
    out = jnp.zeros_like(v)
    for g in range(GMLP_GROUPS):
        m = (lane >= g * HEAD_DIM) & (lane < (g + 1) * HEAD_DIM)
        s = jnp.sum(jnp.where(m, v, 0.0), axis=-1, keepdims=True)
        out = jnp.where(m, s, out)
    return out * (1.0 / HEAD_DIM)


def _inproj_kernel(x_ref, w_ref, cos_ref, sa_ref, sb_ref, lng_ref, lnb_ref,
                   q_ref, k_ref, v_ref, u_ref, vln_ref, pz_ref):
    xb = x_ref[...].astype(bf16)
    cos = cos_ref[...]
    sa = sa_ref[...]
    sb = sb_ref[...]
    for base, out_ref, rope, scale in ((0, q_ref, True, HEAD_DIM ** -0.5),
                                       (ATTN_WIDTH, k_ref, True, 1.0),
                                       (2 * ATTN_WIDTH, v_ref, False, 1.0)):
        y = jnp.dot(xb, w_ref[:, base:base + ATTN_WIDTH], preferred_element_type=f32)
        for p in range(ATTN_WIDTH // LANES):
            yp = y[:, p * LANES:(p + 1) * LANES]
            if rope:
                yp = yp * cos + pltpu.roll(yp, LANES - ROPE_DIM // 2, 1) * sa + pltpu.roll(yp, ROPE_DIM // 2, 1) * sb
            if scale != 1.0:
                yp = yp * scale
            out_ref[p] = yp.astype(bf16)
    o3 = 3 * ATTN_WIDTH
    y = jnp.dot(xb, w_ref[:, o3:o3 + 3 * GMLP_WIDTH], preferred_element_type=f32)
    u_ref[...] = jax.nn.gelu(y[:, :GMLP_WIDTH])
    gv = jax.nn.gelu(y[:, GMLP_WIDTH:2 * GMLP_WIDTH])
    mu = _group_mean(gv)
    d = gv - mu
    var = _group_mean(d * d)
    vln_ref[...] = (d * lax.rsqrt(var + LN_EPS) * lng_ref[...] + lnb_ref[...]).astype(bf16)
    pz_ref[...] = y[:, 2 * GMLP_WIDTH:]


def _inproj(x, w_in, cos_t, sa_t, sb_t, lng, lnb, seq):
    t = x.shape[0]
    nt = t // ROW_TILE
    tiles_per_seq = seq // ROW_TILE
    npair = ATTN_WIDTH // LANES
    tab_spec = pl.BlockSpec((ROW_TILE, LANES), lambda i: (i % tiles_per_seq, 0))
    row = lambda w: pl.BlockSpec((ROW_TILE, w), lambda i: (i, 0))
    const = lambda shape: pl.BlockSpec(shape, lambda i: (0,) * len(shape))
    pair_spec = pl.BlockSpec((npair, ROW_TILE, LANES), lambda i: (0, i, 0))
    pair_shape = jax.ShapeDtypeStruct((npair, t, LANES), bf16)
    return pl.pallas_call(
        _inproj_kernel,
        grid=(nt,),
        in_specs=[row(D_MODEL), const((D_MODEL, IN_COLS)), tab_spec, tab_spec, tab_spec,
                  const((1, GMLP_WIDTH)), const((1, GMLP_WIDTH))],
        out_specs=[pair_spec, pair_spec, pair_spec, row(GMLP_WIDTH), row(GMLP_WIDTH), row(POOL_WIDTH)],
        out_shape=[pair_shape, pair_shape, pair_shape,
                   jax.ShapeDtypeStruct((t, GMLP_WIDTH), f32),
                   jax.ShapeDtypeStruct((t, GMLP_WIDTH), bf16),
                   jax.ShapeDtypeStruct((t, POOL_WIDTH), f32)],
        compiler_params=pltpu.CompilerParams(dimension_semantics=("parallel",), vmem_limit_bytes=VMEM_LIMIT),
        name="inproj",
    )(x, w_in, cos_t, sa_t, sb_t, lng, lnb)


def _attn_kernel(q_ref, k_ref, v_ref, bias_ref, o_ref, qs, ks, vs, o0, o1, o2, l0s, l1s, l2s, *, seq):
    os_ = (o0, o1, o2)
    ls_ = (l0s, l1s, l2s)
    qs[...] = q_ref[...].astype(f32)
    ks[...] = k_ref[...].astype(f32)
    vs[...] = v_ref[...].astype(f32)
    lane = lax.broadcasted_iota(jnp.int32, (ATTN_QBLK, LANES), 1)
    head_a = lane < HEAD_DIM

    for ci, (_, dil) in enumerate(DILATED_CONFIGS):
        length = seq // dil
        nblk = length // ATTN_QBLK
        win = min(2 * ATTN_QBLK, length)
        shift = dil.bit_length() - 1

        def rows(start, size, dil=dil):
            if dil == 1:
                return pl.ds(pl.multiple_of(start, 8), size)
            return pl.ds(start, size, stride=dil)

        def body(u, carry, ci=ci, dil=dil, length=length, win=win, shift=shift, rows=rows):
            c = u & (dil - 1)
            i = u >> shift
            k0 = jnp.clip(i * ATTN_QBLK - ATTN_HALF, 0, length - win)
            delta = k0 - i * ATTN_QBLK
            qrows = rows(c + dil * ATTN_QBLK * i, ATTN_QBLK)
            krows = rows(c + dil * k0, win)
            q = qs[qrows, :]
            kb = ks[krows, :].astype(bf16)
            vb = vs[krows, :].astype(bf16)
            bias = bias_ref[lax.shift_right_logical(-delta, ATTN_HALF.bit_length() - 1), :, 0:win]
            q2 = jnp.concatenate([jnp.where(head_a, q, 0.0), jnp.where(head_a, 0.0, q)], axis=0).astype(bf16)
            s = lax.dot_general(q2, kb, (((1,), (1,)), ((), ())), preferred_element_type=f32) + bias
            m = jnp.max(s, axis=-1, keepdims=True)
            p = jnp.exp(s - m).astype(bf16)
            ol = jnp.dot(p, jnp.concatenate([vb, jnp.ones_like(vb)], axis=1), preferred_element_type=f32)
            l = ol[:, LANES:]
            o = ol[:, :LANES] / l
            lse = m + jnp.log(l)
            os_[ci][qrows, :] = jnp.where(head_a, o[:ATTN_QBLK], o[ATTN_QBLK:])
            ls_[ci][qrows, :] = jnp.where(head_a, lse[:ATTN_QBLK], lse[ATTN_QBLK:])
            return carry

        lax.fori_loop(0, dil * nblk, body, 0, unroll=8)

    chunk = 256

    def combine(j, carry):
        r = pl.ds(pl.multiple_of(j * chunk, chunk), chunk)
        l0, l1, l2 = l0s[r, :], l1s[r, :], l2s[r, :]
        mx = jnp.maximum(jnp.maximum(l0, l1), l2)
        w0, w1, w2 = jnp.exp(l0 - mx), jnp.exp(l1 - mx), jnp.exp(l2 - mx)
        num = w0 * o0[r, :] + w1 * o1[r, :] + w2 * o2[r, :]
        o_ref[r, :] = (num / (w0 + w1 + w2)).astype(bf16)
        return carry

    lax.fori_loop(0, seq // chunk, combine, 0)


def _attention(qp, kp, vp, batch, seq):
    npair = qp.shape[0]
    t = batch * seq
    in_spec = pl.BlockSpec((None, seq, LANES), lambda p, b: (p, b, 0))
    win = 2 * ATTN_QBLK
    rr = np.arange(2 * ATTN_QBLK)[None, :, None] % ATTN_QBLK
    cc = np.arange(win)[None, None, :]
    dd = np.arange(3)[:, None, None] * ATTN_HALF
    bias = jnp.asarray(np.where(np.abs(cc - dd - rr) <= ATTN_HALF, 0.0, NEG_INF), f32)
    return pl.pallas_call(
        functools.partial(_attn_kernel, seq=seq),
        grid=(npair, batch),
        in_specs=[in_spec, in_spec, in_spec, pl.BlockSpec(bias.shape, lambda p, b: (0, 0, 0))],
        out_specs=pl.BlockSpec((seq, LANES), lambda p, b: (b, p)),
        out_shape=jax.ShapeDtypeStruct((t, ATTN_WIDTH), bf16),
        scratch_shapes=[pltpu.VMEM((seq, LANES), f32)] * (3 + 2 * len(DILATED_CONFIGS)),
        compiler_params=pltpu.CompilerParams(dimension_semantics=("parallel", "parallel"),
                                             vmem_limit_bytes=VMEM_LIMIT),
        name="dilated_attn",
    )(qp, kp, vp, bias)


def _mixer_kernel(u_ref, vln_ref, pz_ref, prev_ref, next_ref, ws_ref, bs_ref, wp_ref, sc_ref,
                  gm_ref, pool_ref, z0, z1, z2, z3, z4, *, seq):
    tm = ROW_TILE
    h = POOL_HALO
    lane = lax.broadcasted_iota(jnp.int32, (GMLP_CHUNK, GMLP_WIDTH), 1)
    for cc in range(tm // GMLP_CHUNK):
        r = slice(cc * GMLP_CHUNK, (cc + 1) * GMLP_CHUNK)
        vc = vln_ref[r, :]
        sv = jnp.zeros((GMLP_CHUNK, GMLP_WIDTH), f32)
        for g in range(GMLP_GROUPS):
            svg = jnp.dot(ws_ref[g], vc, preferred_element_type=f32)
            sv = jnp.where((lane >= g * HEAD_DIM) & (lane < (g + 1) * HEAD_DIM), svg, sv)
        gm_ref[r, :] = (u_ref[r, :] * (sv + bs_ref[...])).astype(bf16)

    i = pl.program_id(0)
    tiles_per_seq = seq // tm
    pos0 = (i % tiles_per_seq) * tm
    first = pos0 == 0
    last = pos0 + tm == seq
    z0[0:h, :] = jnp.where(first, 0.0, prev_ref[...])
    z0[h:h + tm, :] = pz_ref[...]
    z0[h + tm:h + tm + h, :] = jnp.where(last, 0.0, next_ref[...])
    z1[8:tm + 2 * h - 8, :] = z0[8:tm + 2 * h - 8, :] + z0[7:tm + 2 * h - 9, :]
    z2[16:tm + 2 * h - 16, :] = z1[15:tm + 2 * h - 17, :] + z1[17:tm + 2 * h - 15, :]
    z3[24:tm + 2 * h - 24, :] = z2[22:tm + 2 * h - 26, :] + z2[26:tm + 2 * h - 22, :]
    z4[h:h + tm, :] = z3[h - 4:h + tm - 4, :] + z3[h + 4:h + tm + 4, :]
    lane_t = lax.broadcasted_iota(jnp.int32, (tm, POOL_WIDTH), 1)
    pos = pos0 + lax.broadcasted_iota(jnp.int32, (tm, POOL_WIDTH), 0)
    zc = z0[h:h + tm, :]
    pooled = jnp.zeros((tm, POOL_WIDTH), f32)
    for g, (w, zw) in enumerate(zip(POOL_WINDOWS, (z1, z2, z3, z4))):
        left = w // 2
        right = w - 1 - left
        lo = jnp.maximum(pos - left, 0)
        hi = jnp.minimum(pos + right + 1, seq)
        cnt = (hi - lo).astype(f32)
        val = zw[h:h + tm, :] / cnt - zc
        pooled = jnp.where((lane_t >= g * HEAD_DIM) & (lane_t < (g + 1) * HEAD_DIM), val, pooled)
    y = jnp.dot(pooled.astype(bf16), wp_ref[...], preferred_element_type=f32)
    pool_ref[...] = (y * sc_ref[...]).astype(bf16)


def _mixer(u, vln, pz, ws, bs_full, wp_bd, scale, seq):
    t = u.shape[0]
    tm, h = ROW_TILE, POOL_HALO
    nt = t // tm
    hb = tm // h
    row = lambda w: pl.BlockSpec((tm, w), lambda i: (i, 0))
    const = lambda shape: pl.BlockSpec(shape, lambda i: (0,) * len(shape))
    prev_spec = pl.BlockSpec((h, POOL_WIDTH), lambda i: (jnp.maximum(i * hb - 1, 0), 0))
    next_spec = pl.BlockSpec((h, POOL_WIDTH), lambda i: (jnp.minimum((i + 1) * hb, t // h - 1), 0))
    zbuf = pltpu.VMEM((tm + 2 * h, POOL_WIDTH), f32)
    return pl.pallas_call(
        functools.partial(_mixer_kernel, seq=seq),
        grid=(nt,),
        in_specs=[row(GMLP_WIDTH), row(GMLP_WIDTH), row(POOL_WIDTH), prev_spec, next_spec,
                  const((GMLP_GROUPS, GMLP_CHUNK, GMLP_CHUNK)), const((GMLP_CHUNK, GMLP_WIDTH)),
                  const((POOL_WIDTH, POOL_WIDTH)), const((1, POOL_WIDTH))],
        out_specs=[row(GMLP_WIDTH), row(POOL_WIDTH)],
        out_shape=[jax.ShapeDtypeStruct((t, GMLP_WIDTH), bf16), jax.ShapeDtypeStruct((t, POOL_WIDTH), bf16)],
        scratch_shapes=[zbuf] * 5,
        compiler_params=pltpu.CompilerParams(dimension_semantics=("parallel",), vmem_limit_bytes=VMEM_LIMIT),
        name="mixer",
    )(u, vln, pz, pz, pz, ws, bs_full, wp_bd, scale)


def _top2_of4(b, s):
    v1, i1, s1 = b[0], jnp.zeros(b[0].shape, jnp.int32), s[0]
    for j in range(1, 4):
        gt = b[j] > v1
        v1 = jnp.where(gt, b[j], v1)
        i1 = jnp.where(gt, j, i1)
        s1 = jnp.where(gt, s[j], s1)
    v2 = jnp.full(b[0].shape, -jnp.inf, f32)
    i2 = jnp.full(b[0].shape, -1, jnp.int32)
    s2 = jnp.zeros(b[0].shape, f32)
    for j in range(4):
        cand = jnp.where(i1 != j, b[j], -jnp.inf) > v2
        v2 = jnp.where(cand, b[j], v2)
        i2 = jnp.where(cand, j, i2)
        s2 = jnp.where(cand, s[j], s2)
    return v1 + v2, i1, i2, s1, s2


def _outproj_kernel(attn_ref, gm_ref, pool_ref, x_ref, w_ref, g_ref, b_ref, rw_ref, rb_ref, tri_ref,
                    xe_ref, meta_ref, cnt_ref, carry):
    tm = ROW_TILE

    @pl.when(pl.program_id(0) == 0)
    def _():
        carry[...] = jnp.zeros_like(carry)

    o1, o2 = ATTN_WIDTH, ATTN_WIDTH + GMLP_WIDTH
    hmix = jnp.dot(attn_ref[...], w_ref[0:o1, :], preferred_element_type=f32)
    hmix = hmix + jnp.dot(gm_ref[...], w_ref[o1:o2, :], preferred_element_type=f32)
    hmix = hmix + jnp.dot(pool_ref[...], w_ref[o2:, :], preferred_element_type=f32)
    x1 = _layernorm(DEEPNORM_ALPHA * x_ref[...] + hmix, g_ref[...], b_ref[...])
    xe_ref[:, 0:D_MODEL] = x1

    logits = lax.dot_general(rw_ref[...], x1.astype(bf16), (((1,), (1,)), ((), ())),
                             preferred_element_type=f32)
    scores = 1.0 / (1.0 + jnp.exp(-logits))
    biased = scores + rb_ref[...]
    brow = [biased[e:e + 1, :] for e in range(N_EXPERTS)]
    srow = [scores[e:e + 1, :] for e in range(N_EXPERTS)]
    best = None
    for g in range(N_EXPERT_GROUPS):
        sl = slice(g * EXPERTS_PER_GROUP, (g + 1) * EXPERTS_PER_GROUP)
        gs, i1, i2, s1, s2 = _top2_of4(brow[sl], srow[sl])
        if best is None:
            best = (gs, jnp.zeros(gs.shape, jnp.int32), i1, i2, s1, s2)
        else:
            gt = gs > best[0]
            best = (jnp.where(gt, gs, best[0]), jnp.where(gt, g, best[1]), jnp.where(gt, i1, best[2]),
                    jnp.where(gt, i2, best[3]), jnp.where(gt, s1, best[4]), jnp.where(gt, s2, best[5]))
    _, gsel, i1, i2, s1, s2 = best
    den = s1 + s2
    ga, gb = s1 / den, s2 / den
    first_lo = i1 < i2
    lo = jnp.minimum(i1, i2)
    hi = jnp.maximum(i1, i2)
    g_lo = jnp.where(first_lo, ga, gb)
    g_hi = jnp.where(first_lo, gb, ga)
    pidx = jnp.where(lo == 0, hi - 1, jnp.where(lo == 1, hi + 1, N_PAIRS - 1))
    cls = gsel * N_PAIRS + pidx

    onehot = lax.broadcasted_iota(jnp.int32, (CLASS_ROWS, tm), 0) == cls
    ohf = jnp.where(onehot, 1.0, 0.0)
    prefix = jnp.dot(ohf.astype(bf16), tri_ref[...], preferred_element_type=f32)
    base = carry[:, 0:1]
    rank = jnp.sum(ohf * (prefix + base), axis=0, keepdims=True)
    new_carry = carry[...] + jnp.sum(ohf, axis=1, keepdims=True)
    carry[...] = new_carry
    cnt_ref[...] = new_carry

    srow_id = lax.broadcasted_iota(jnp.int32, (8, tm), 0)
    meta_ref[...] = jnp.where(srow_id == 0, cls, jnp.where(srow_id == 1, rank.astype(jnp.int32), 0))
    side8 = jnp.where(srow_id == 0, g_lo, jnp.where(srow_id == 1, g_hi, 0.0))
    side = jnp.concatenate([side8, jnp.zeros((SIDE - 8, tm), f32)], axis=0)
    xe_ref[:, D_MODEL:] = side.T


def _outproj(attn, gm, pool, x, w_out, g, b, rw_pad, rb, tri):
    t = x.shape[0]
    tm = ROW_TILE
    row = lambda w: pl.BlockSpec((tm, w), lambda i: (i, 0))
    const = lambda shape: pl.BlockSpec(shape, lambda i: (0,) * len(shape))
    return pl.pallas_call(
        _outproj_kernel,
        grid=(t // tm,),
        in_specs=[row(ATTN_WIDTH), row(GMLP_WIDTH), row(POOL_WIDTH), row(D_MODEL), const((D_MODEL, D_MODEL)),
                  const((1, D_MODEL)), const((1, D_MODEL)), const((N_EXPERTS, D_MODEL)), const((N_EXPERTS, 1)),
                  const((tm, tm))],
        out_specs=[row(D_MODEL + SIDE), pl.BlockSpec((None, 8, tm), lambda i: (i, 0, 0)),
                   const((CLASS_ROWS, LANES))],
        out_shape=[jax.ShapeDtypeStruct((t, D_MODEL + SIDE), f32), jax.ShapeDtypeStruct((t // tm, 8, tm), jnp.int32),
                   jax.ShapeDtypeStruct((CLASS_ROWS, LANES), f32)],
        scratch_shapes=[pltpu.VMEM((CLASS_ROWS, LANES), f32)],
        compiler_params=pltpu.CompilerParams(dimension_semantics=("arbitrary",), vmem_limit_bytes=VMEM_LIMIT),
        name="outproj_ln_router",
    )(attn, gm, pool, x, w_out, g, b, rw_pad, rb, tri)


def _permute_kernel(idx_ref, zflag_ref, src_ref, *rest, scatter, nzero):
    if scatter:
        dst_ref, zeros, sem, zsem = rest
    else:
        dst_ref, sem = rest
    step = pl.program_id(0)

    if scatter:
        def zcopy(j):
            return pltpu.make_async_copy(zeros, dst_ref.at[pl.ds(pl.multiple_of(j * MOE_BLK, MOE_BLK), MOE_BLK), :],
                                         zsem)

        @pl.when(step == 0)
        def _():
            zeros[...] = jnp.zeros_like(zeros)

            def zstart(j, c):
                @pl.when(zflag_ref[j] != 0)
                def _():
                    zcopy(j).start()
                return c

            def zwait(j, c):
                @pl.when(zflag_ref[j] != 0)
                def _():
                    zcopy(j).wait()
                return c

            lax.fori_loop(0, nzero, zstart, 0)
            lax.fori_loop(0, nzero, zwait, 0)

    def row_copy(r):
        other = idx_ref[0, 0, r]
        if scatter:
            return pltpu.make_async_copy(src_ref.at[pl.ds(r, 1), :], dst_ref.at[pl.ds(other, 1), :], sem)
        return pltpu.make_async_copy(src_ref.at[pl.ds(other, 1), :], dst_ref.at[pl.ds(r, 1), :], sem)

    def start(g, c):
        for k in range(PERM_UNROLL):
            row_copy(g * PERM_UNROLL + k).start(priority=k % 2)
        return c

    lax.fori_loop(0, PERM_CHUNK // PERM_UNROLL, start, 0)
    if scatter:
        pltpu.make_async_copy(src_ref, dst_ref.at[pl.ds(0, PERM_CHUNK), :], sem).wait()
    else:
        pltpu.make_async_copy(src_ref.at[pl.ds(0, PERM_CHUNK), :], dst_ref, sem).wait()


def _permute(idx, zflag, src, n_dst, width, scatter):
    nsteps = idx.shape[0]
    nzero = zflag.shape[0]
    scratch = [pltpu.SemaphoreType.DMA(())]
    if scatter:
        scratch = [pltpu.VMEM((MOE_BLK, width), f32), pltpu.SemaphoreType.DMA(()), pltpu.SemaphoreType.DMA(())]
    block = pl.BlockSpec((PERM_CHUNK, width), lambda i: (i, 0))
    hbm = pl.BlockSpec(memory_space=pl.ANY)
    smem = pl.BlockSpec(memory_space=pltpu.SMEM)
    return pl.pallas_call(
        functools.partial(_permute_kernel, scatter=scatter, nzero=nzero),
        grid=(nsteps,),
        in_specs=[pl.BlockSpec((1, 1, PERM_CHUNK), lambda i: (i, 0, 0), memory_space=pltpu.SMEM), smem,
                  block if scatter else hbm],
        out_specs=hbm if scatter else block,
        out_shape=jax.ShapeDtypeStruct((n_dst, width), f32),
        scratch_shapes=scratch,
        compiler_params=pltpu.CompilerParams(dimension_semantics=("arbitrary",), vmem_limit_bytes=VMEM_LIMIT),
        name="dispatch_rows" if scatter else "undispatch_rows",
    )(idx, zflag, src)


def _expert_kernel(elo_ref, ehi_ref, nused_ref, new_ref, xb_ref, wg_lo, wu_lo, wd_lo, wg_hi, wu_hi, wd_hi,
                   g_ref, b_ref, out_ref, *wb):
    j = pl.program_id(0)

    @pl.when(new_ref[j] != 0)
    def _():
        for src, dst in zip((wg_lo, wu_lo, wd_lo, wg_hi, wu_hi, wd_hi), wb):
            dst[...] = src[...].astype(bf16)

    @pl.when(j < nused_ref[0])
    def _():
        x = xb_ref[:, 0:D_MODEL]
        gates = xb_ref[:, D_MODEL:D_MODEL + SIDE]
        xb = x.astype(bf16)
        moe = jnp.zeros_like(x)
        for col, (wg, wu, wd) in enumerate((wb[0:3], wb[3:6])):
            a = jnp.dot(xb, wg[...], preferred_element_type=f32)
            up = jnp.dot(xb, wu[...], preferred_element_type=f32)
            hidden = (a / (1.0 + jnp.exp(-a))) * up
            y = jnp.dot(hidden.astype(bf16), wd[...], preferred_element_type=f32)
            moe = moe + y * gates[:, col:col + 1]
        out_ref[...] = _layernorm(DEEPNORM_ALPHA * x + moe, g_ref[...], b_ref[...])

    @pl.when(j >= nused_ref[0])
    def _():
        out_ref[...] = jnp.zeros_like(out_ref)


def _experts(elo, ehi, nused, newcls, buf, wg, wu, wd, g, b, layer):
    p = buf.shape[0]
    nb = p // MOE_BLK
    wspec = lambda shape, which: pl.BlockSpec(
        (None, None) + shape, (lambda j, elo, ehi, nu, nw: (layer, elo[j], 0, 0)) if which == 0
        else (lambda j, elo, ehi, nu, nw: (layer, ehi[j], 0, 0)))
    up_shape, down_shape = (D_MODEL, EXPERT_FF), (EXPERT_FF, D_MODEL)
    const = lambda shape: pl.BlockSpec(shape, lambda j, elo, ehi, nu, nw: (0,) * len(shape))
    grid_spec = pltpu.PrefetchScalarGridSpec(
        num_scalar_prefetch=4,
        grid=(nb,),
        in_specs=[pl.BlockSpec((MOE_BLK, D_MODEL + SIDE), lambda j, elo, ehi, nu, nw: (j, 0)),
                  wspec(up_shape, 0), wspec(up_shape, 0), wspec(down_shape, 0),
                  wspec(up_shape, 1), wspec(up_shape, 1), wspec(down_shape, 1),
                  const((1, D_MODEL)), const((1, D_MODEL))],
        out_specs=pl.BlockSpec((MOE_BLK, D_MODEL), lambda j, elo, ehi, nu, nw: (j, 0)),
        scratch_shapes=[pltpu.VMEM(up_shape, bf16), pltpu.VMEM(up_shape, bf16), pltpu.VMEM(down_shape, bf16)] * 2,
    )
    return pl.pallas_call(
        _expert_kernel,
        grid_spec=grid_spec,
        out_shape=jax.ShapeDtypeStruct((p, D_MODEL), f32),
        compiler_params=pltpu.CompilerParams(dimension_semantics=("arbitrary",), vmem_limit_bytes=VMEM_LIMIT),
        name="expert_pairs",
    )(elo, ehi, nused, newcls, buf, wg, wu, wd, wg, wu, wd, g, b)


def _rope_tables(seq):
    half = ROPE_DIM // 2
    inv_freq = jnp.power(jnp.float32(ROPE_THETA), -jnp.arange(half, dtype=f32) / half)
    ang = jnp.arange(seq, dtype=f32)[:, None] * inv_freq[None, :]
    lane = np.arange(LANES) % HEAD_DIM
    fidx = lane % half
    in_rope = lane < ROPE_DIM
    first = lane < half
    cos_l = jnp.cos(ang)[:, fidx]
    sin_l = jnp.sin(ang)[:, fidx]
    cos_t = jnp.where(in_rope[None, :], cos_l, 1.0)
    sa_t = jnp.where((in_rope & first)[None, :], -sin_l, 0.0)
    sb_t = jnp.where((in_rope & ~first)[None, :], sin_l, 0.0)
    return cos_t, sa_t, sb_t


_PAIRS = [(a, b) for a in range(EXPERTS_PER_GROUP) for b in range(a + 1, EXPERTS_PER_GROUP)]


def _dispatch_plan(counts, meta, t):
    counts = counts.astype(jnp.int32)
    padded = ((counts + MOE_BLK - 1) // MOE_BLK) * MOE_BLK
    pend = jnp.cumsum(padded)
    pstart = pend - padded
    nb = (t + N_CLASSES * MOE_BLK) // MOE_BLK
    blk_start = jnp.arange(nb, dtype=jnp.int32) * MOE_BLK
    blk_cls = jnp.minimum(jnp.sum((blk_start[:, None] >= pend[None, :]).astype(jnp.int32), axis=1), N_CLASSES - 1)
    valid_end = pstart[blk_cls] + counts[blk_cls]
    zflag = ((blk_start + MOE_BLK > valid_end) | (blk_start >= pend[-1])).astype(jnp.int32)
    pair_lo = jnp.asarray([p[0] for p in _PAIRS], jnp.int32)
    pair_hi = jnp.asarray([p[1] for p in _PAIRS], jnp.int32)
    grp = blk_cls // N_PAIRS
    elo = grp * EXPERTS_PER_GROUP + pair_lo[blk_cls % N_PAIRS]
    ehi = grp * EXPERTS_PER_GROUP + pair_hi[blk_cls % N_PAIRS]
    nused = (pend[-1] // MOE_BLK).astype(jnp.int32).reshape(1)
    newcls = jnp.concatenate([jnp.ones((1,), jnp.int32), (blk_cls[1:] != blk_cls[:-1]).astype(jnp.int32)])
    cls, rank = meta[:, 0:1, :], meta[:, 1:2, :]
    dest = rank
    for c in range(N_CLASSES):
        dest = dest + jnp.where(cls == c, pstart[c], 0)
    return dest, zflag, elo, ehi, nused, newcls


def kernel(x, w_in, w_out, gmlp_ln_g, gmlp_ln_b, gmlp_w_s, gmlp_b_s, pool_w, pool_scale, ln1_g, ln1_b,
           router_w, router_bias, w_gate, w_up, w_down, ln2_g, ln2_b):
    batch, seq, d = x.shape
    t = batch * seq
    assert d == D_MODEL and seq % ROW_TILE == 0 and t % PERM_CHUNK == 0
    cos_t, sa_t, sb_t = _rope_tables(seq)
    tri = (np.arange(ROW_TILE)[:, None] < np.arange(ROW_TILE)[None, :])
    tri = jnp.asarray(tri, bf16)
    rw_pad = router_w.T.astype(bf16)
    rb = router_bias.reshape(N_EXPERTS, 1)
    n_rows = t + N_CLASSES * MOE_BLK
    dummy_flag = jnp.zeros((1,), jnp.int32)

    xf = x.reshape(t, d)
    for l in range(DEPTH):
        lng = gmlp_ln_g[l].reshape(1, GMLP_WIDTH)
        lnb = gmlp_ln_b[l].reshape(1, GMLP_WIDTH)
        qp, kp, vp, u, vln, pz = _inproj(xf, w_in[l].astype(bf16), cos_t, sa_t, sb_t, lng, lnb, seq)
        attn = _attention(qp, kp, vp, batch, seq)
        bs_full = jnp.repeat(gmlp_b_s[l].T, HEAD_DIM, axis=1)
        wp_bd = jnp.zeros((POOL_WIDTH, POOL_WIDTH), f32)
        for g in range(len(POOL_WINDOWS)):
            wp_bd = wp_bd.at[g * HEAD_DIM:(g + 1) * HEAD_DIM, g * HEAD_DIM:(g + 1) * HEAD_DIM].set(pool_w[l, g])
        wp_bd = wp_bd.astype(bf16)
        gm, pool = _mixer(u, vln, pz, gmlp_w_s[l].astype(bf16), bs_full, wp_bd,
                          pool_scale[l].reshape(1, POOL_WIDTH), seq)
        xe, meta, counts = _outproj(attn, gm, pool, xf, w_out[l].astype(bf16), ln1_g[l].reshape(1, d),
                                    ln1_b[l].reshape(1, d), rw_pad, rb, tri)
        dest, zflag, elo, ehi, nused, newcls = _dispatch_plan(counts[:N_CLASSES, 0], meta, t)
        dest = dest.reshape(t // PERM_CHUNK, 1, PERM_CHUNK)
        buf = _permute(dest, zflag, xe, n_rows, D_MODEL + SIDE, scatter=True)
        ys = _experts(elo, ehi, nused, newcls, buf, w_gate, w_up, w_down,
                      ln2_g[l].reshape(1, d), ln2_b[l].reshape(1, d), l)
        xf = _permute(dest, dummy_flag, ys, t, D_MODEL, scatter=False)
    return xf.reshape(batch, seq, d)
```

```python
import functools

import jax
import jax.numpy as jnp
import numpy as np
from jax import lax
from jax.experimental import pallas as pl
from jax.experimental.pallas import tpu as pltpu

f32 = jnp.float32
bf16 = jnp.bfloat16

D_MODEL = 1024
HEAD_DIM = 64
ATTN_WIDTH = 512
DILATED_CONFIGS = ((128, 1), (512, 4), (2048, 16))
ATTN_HALF = 64
ROPE_THETA = 500000.0
ROPE_DIM = 16
GMLP_WIDTH = 256
GMLP_GROUPS = 4
GMLP_CHUNK = 128
POOL_WIDTH = 256
POOL_WINDOWS = (2, 4, 8, 16)
IN_COLS = 3 * ATTN_WIDTH + 2 * GMLP_WIDTH + POOL_WIDTH
N_EXPERTS = 16
N_EXPERT_GROUPS = 4
EXPERTS_PER_GROUP = 4
EXPERT_FF = 512
DEPTH = 2
DEEPNORM_ALPHA = float((2 * DEPTH) ** 0.25)
LN_EPS = 1e-5
NEG_INF = -1e30

LANES = 128
ROW_TILE = 512
ATTN_QBLK = 128
POOL_HALO = 32
N_PAIRS = 6
N_CLASSES = N_EXPERT_GROUPS * N_PAIRS
CLASS_ROWS = 32
MOE_BLK = 256
SIDE = LANES
PERM_CHUNK = 1024
PERM_UNROLL = 16
VMEM_LIMIT = 48 * 1024 * 1024


def _layernorm(y, g, b):
    mu = jnp.mean(y, axis=-1, keepdims=True)
    d = y - mu
    var = jnp.mean(d * d, axis=-1, keepdims=True)
    return d * lax.rsqrt(var + LN_EPS) * g + b


def _group_mean(v):
    lane = lax.broadcasted_iota(jnp.int32, v.shape, 1)
    out = jnp.zeros_like(v)
    for g in range(GMLP_GROUPS):
        m = (lane >= g * HEAD_DIM) & (lane < (g + 1) * HEAD_DIM)
        s = jnp.sum(jnp.where(m, v, 0.0), axis=-1, keepdims=True)
        out = jnp.where(m, s, out)
    return out * (1.0 / HEAD_DIM)


def _inproj_kernel(x_ref, w_ref, cos_ref, sa_ref, sb_ref, lng_ref, lnb_ref,
                   q_ref, k_ref, v_ref, u_ref, vln_ref, pz_ref):
    xb = x_ref[...].astype(bf16)
    cos = cos_ref[...]
    sa = sa_ref[...]
    sb = sb_ref[...]
    for base, out_ref, rope, scale in ((0, q_ref, True, HEAD_DIM ** -0.5),
                                       (ATTN_WIDTH, k_ref, True, 1.0),
                                       (2 * ATTN_WIDTH, v_ref, False, 1.0)):
        y = jnp.dot(xb, w_ref[:, base:base + ATTN_WIDTH], preferred_element_type=f32)
        for p in range(ATTN_WIDTH // LANES):
            yp = y[:, p * LANES:(p + 1) * LANES]
            if rope:
                yp = yp * cos + pltpu.roll(yp, LANES - ROPE_DIM // 2, 1) * sa + pltpu.roll(yp, ROPE_DIM // 2, 1) * sb
            if scale != 1.0:
                yp = yp * scale
            out_ref[p] = yp.astype(bf16)
    o3 = 3 * ATTN_WIDTH
    y = jnp.dot(xb, w_ref[:, o3:o3 + 3 * GMLP_WIDTH], preferred_element_type=f32)
    u_ref[...] = jax.nn.gelu(y[:, :GMLP_WIDTH])
    gv = jax.nn.gelu(y[:, GMLP_WIDTH:2 * GMLP_WIDTH])
    mu = _group_mean(gv)
    d = gv - mu
    var = _group_mean(d * d)
    vln_ref[...] = (d * lax.rsqrt(var + LN_EPS) * lng_ref[...] + lnb_ref[...]).astype(bf16)
    pz_ref[...] = y[:, 2 * GMLP_WIDTH:]


def _inproj(x, w_in, cos_t, sa_t, sb_t, lng, lnb, seq):
    t = x.shape[0]
    nt = t // ROW_TILE
    tiles_per_seq = seq // ROW_TILE
    npair = ATTN_WIDTH // LANES
    tab_spec = pl.BlockSpec((ROW_TILE, LANES), lambda i: (i % tiles_per_seq, 0))
    row = lambda w: pl.BlockSpec((ROW_TILE, w), lambda i: (i, 0))
    const = lambda shape: pl.BlockSpec(shape, lambda i: (0,) * len(shape))
    pair_spec = pl.BlockSpec((npair, ROW_TILE, LANES), lambda i: (0, i, 0))
    pair_shape = jax.ShapeDtypeStruct((npair, t, LANES), bf16)
    return pl.pallas_call(
        _inproj_kernel,
        grid=(nt,),
        in_specs=[row(D_MODEL), const((D_MODEL, IN_COLS)), tab_spec, tab_spec, tab_spec,
                  const((1, GMLP_WIDTH)), const((1, GMLP_WIDTH))],
        out_specs=[pair_spec, pair_spec, pair_spec, row(GMLP_WIDTH), row(GMLP_WIDTH), row(POOL_WIDTH)],
        out_shape=[pair_shape, pair_shape, pair_shape,
                   jax.ShapeDtypeStruct((t, GMLP_WIDTH), f32),
                   jax.ShapeDtypeStruct((t, GMLP_WIDTH), bf16),
                   jax.ShapeDtypeStruct((t, POOL_WIDTH), f32)],
        compiler_params=pltpu.CompilerParams(dimension_semantics=("parallel",), vmem_limit_bytes=VMEM_LIMIT),
        name="inproj",
    )(x, w_in, cos_t, sa_t, sb_t, lng, lnb)


def _attn_kernel(q_ref, k_ref, v_ref, bias_ref, o_ref, qs, ks, vs, o0, o1, o2, l0s, l1s, l2s, *, seq):
    os_ = (o0, o1, o2)
    ls_ = (l0s, l1s, l2s)
    qs[...] = q_ref[...].astype(f32)
    ks[...] = k_ref[...].astype(f32)
    vs[...] = v_ref[...].astype(f32)
    lane = lax.broadcasted_iota(jnp.int32, (ATTN_QBLK, LANES), 1)
    head_a = lane < HEAD_DIM

    for ci, (_, dil) in enumerate(DILATED_CONFIGS):
        length = seq // dil
        nblk = length // ATTN_QBLK
        win = min(2 * ATTN_QBLK, length)
        shift = dil.bit_length() - 1

        def rows(start, size, dil=dil):
            if dil == 1:
                return pl.ds(pl.multiple_of(start, 8), size)
            return pl.ds(start, size, stride=dil)

        def body(u, carry, ci=ci, dil=dil, length=length, win=win, shift=shift, rows=rows):
            c = u & (dil - 1)
            i = u >> shift
            k0 = jnp.clip(i * ATTN_QBLK - ATTN_HALF, 0, length - win)
            delta = k0 - i * ATTN_QBLK
            qrows = rows(c + dil * ATTN_QBLK * i, ATTN_QBLK)
            krows = rows(c + dil * k0, win)
            q = qs[qrows, :]
            kb = ks[krows, :].astype(bf16)
            vb = vs[krows, :].astype(bf16)
            bias = bias_ref[lax.shift_right_logical(-delta, ATTN_HALF.bit_length() - 1), :, 0:win]
            q2 = jnp.concatenate([jnp.where(head_a, q, 0.0), jnp.where(head_a, 0.0, q)], axis=0).astype(bf16)
            s = lax.dot_general(q2, kb, (((1,), (1,)), ((), ())), preferred_element_type=f32) + bias
            m = jnp.max(s, axis=-1, keepdims=True)
            p = jnp.exp(s - m).astype(bf16)
            ol = jnp.dot(p, jnp.concatenate([vb, jnp.ones_like(vb)], axis=1), preferred_element_type=f32)
            l = ol[:, LANES:]
            o = ol[:, :LANES] / l
            lse = m + jnp.log(l)
            os_[ci][qrows, :] = jnp.where(head_a, o[:ATTN_QBLK], o[ATTN_QBLK:])
            ls_[ci][qrows, :] = jnp.where(head_a, lse[:ATTN_QBLK], lse[ATTN_QBLK:])
            return carry

        lax.fori_loop(0, dil * nblk, body, 0, unroll=8)

    chunk = 256

    def combine(j, carry):
        r = pl.ds(pl.multiple_of(j * chunk, chunk), chunk)
        l0, l1, l2 = l0s[r, :], l1s[r, :], l2s[r, :]
        mx = jnp.maximum(jnp.maximum(l0, l1), l2)
        w0, w1, w2 = jnp.exp(l0 - mx), jnp.exp(l1 - mx), jnp.exp(l2 - mx)
        num = w0 * o0[r, :] + w1 * o1[r, :] + w2 * o2[r, :]
        o_ref[r, :] = (num / (w0 + w1 + w2)).astype(bf16)
        return carry

    lax.fori_loop(0, seq // chunk, combine, 0)


def _attention(qp, kp, vp, batch, seq):
    npair = qp.shape[0]
    t = batch * seq
    in_spec = pl.BlockSpec((None, seq, LANES), lambda p, b: (p, b, 0))
    win = 2 * ATTN_QBLK
    rr = np.arange(2 * ATTN_QBLK)[None, :, None] % ATTN_QBLK
    cc = np.arange(win)[None, None, :]
    dd = np.arange(3)[:, None, None] * ATTN_HALF
    bias = jnp.asarray(np.where(np.abs(cc - dd - rr) <= ATTN_HALF, 0.0, NEG_INF), f32)
    return pl.pallas_call(
        functools.partial(_attn_kernel, seq=seq),
        grid=(npair, batch),
        in_specs=[in_spec, in_spec, in_spec, pl.BlockSpec(bias.shape, lambda p, b: (0, 0, 0))],
        out_specs=pl.BlockSpec((seq, LANES), lambda p, b: (b, p)),
        out_shape=jax.ShapeDtypeStruct((t, ATTN_WIDTH), bf16),
        scratch_shapes=[pltpu.VMEM((seq, LANES), f32)] * (3 + 2 * len(DILATED_CONFIGS)),
        compiler_params=pltpu.CompilerParams(dimension_semantics=("parallel", "parallel"),
                                             vmem_limit_bytes=VMEM_LIMIT),
        name="dilated_attn",
    )(qp, kp, vp, bias)


def _mixer_kernel(u_ref, vln_ref, pz_ref, prev_ref, next_ref, ws_ref, bs_ref, wp_ref, sc_ref,
                  gm_ref, pool_ref, z0, z1, z2, z3, z4, *, seq):
    tm = ROW_TILE
    h = POOL_HALO
    lane = lax.broadcasted_iota(jnp.int32, (GMLP_CHUNK, GMLP_WIDTH), 1)
    for cc in range(tm // GMLP_CHUNK):
        r = slice(cc * GMLP_CHUNK, (cc + 1) * GMLP_CHUNK)
        vc = vln_ref[r, :]
        sv = jnp.zeros((GMLP_CHUNK, GMLP_WIDTH), f32)
        for g in range(GMLP_GROUPS):
            svg = jnp.dot(ws_ref[g], vc, preferred_element_type=f32)
            sv = jnp.where((lane >= g * HEAD_DIM) & (lane < (g + 1) * HEAD_DIM), svg, sv)
        gm_ref[r, :] = (u_ref[r, :] * (sv + bs_ref[...])).astype(bf16)

    i = pl.program_id(0)
    tiles_per_seq = seq // tm
    pos0 = (i % tiles_per_seq) * tm
    first = pos0 == 0
    last = pos0 + tm == seq
    z0[0:h, :] = jnp.where(first, 0.0, prev_ref[...])
    z0[h:h + tm, :] = pz_ref[...]
    z0[h + tm:h + tm + h, :] = jnp.where(last, 0.0, next_ref[...])
    z1[8:tm + 2 * h - 8, :] = z0[8:tm + 2 * h - 8, :] + z0[7:tm + 2 * h - 9, :]
    z2[16:tm + 2 * h - 16, :] = z1[15:tm + 2 * h - 17, :] + z1[17:tm + 2 * h - 15, :]
    z3[24:tm + 2 * h - 24, :] = z2[22:tm + 2 * h - 26, :] + z2[26:tm + 2 * h - 22, :]
    z4[h:h + tm, :] = z3[h - 4:h + tm - 4, :] + z3[h + 4:h + tm + 4, :]
    lane_t = lax.broadcasted_iota(jnp.int32, (tm, POOL_WIDTH), 1)
    pos = pos0 + lax.broadcasted_iota(jnp.int32, (tm, POOL_WIDTH), 0)
    zc = z0[h:h + tm, :]
    pooled = jnp.zeros((tm, POOL_WIDTH), f32)
    for g, (w, zw) in enumerate(zip(POOL_WINDOWS, (z1, z2, z3, z4))):
        left = w // 2
        right = w - 1 - left
        lo = jnp.maximum(pos - left, 0)
        hi = jnp.minimum(pos + right + 1, seq)
        cnt = (hi - lo).astype(f32)
        val = zw[h:h + tm, :] / cnt - zc
        pooled = jnp.where((lane_t >= g * HEAD_DIM) & (lane_t < (g + 1) * HEAD_DIM), val, pooled)
    y = jnp.dot(pooled.astype(bf16), wp_ref[...], preferred_element_type=f32)
    pool_ref[...] = (y * sc_ref[...]).astype(bf16)


def _mixer(u, vln, pz, ws, bs_full, wp_bd, scale, seq):
    t = u.shape[0]
    tm, h = ROW_TILE, POOL_HALO
    nt = t // tm
    hb = tm // h
    row = lambda w: pl.BlockSpec((tm, w), lambda i: (i, 0))
    const = lambda shape: pl.BlockSpec(shape, lambda i: (0,) * len(shape))
    prev_spec = pl.BlockSpec((h, POOL_WIDTH), lambda i: (jnp.maximum(i * hb - 1, 0), 0))
    next_spec = pl.BlockSpec((h, POOL_WIDTH), lambda i: (jnp.minimum((i + 1) * hb, t // h - 1), 0))
    zbuf = pltpu.VMEM((tm + 2 * h, POOL_WIDTH), f32)
    return pl.pallas_call(
        functools.partial(_mixer_kernel, seq=seq),
        grid=(nt,),
        in_specs=[row(GMLP_WIDTH), row(GMLP_WIDTH), row(POOL_WIDTH), prev_spec, next_spec,
                  const((GMLP_GROUPS, GMLP_CHUNK, GMLP_CHUNK)), const((GMLP_CHUNK, GMLP_WIDTH)),
                  const((POOL_WIDTH, POOL_WIDTH)), const((1, POOL_WIDTH))],
        out_specs=[row(GMLP_WIDTH), row(POOL_WIDTH)],
        out_shape=[jax.ShapeDtypeStruct((t, GMLP_WIDTH), bf16), jax.ShapeDtypeStruct((t, POOL_WIDTH), bf16)],
        scratch_shapes=[zbuf] * 5,
        compiler_params=pltpu.CompilerParams(dimension_semantics=("parallel",), vmem_limit_bytes=VMEM_LIMIT),
        name="mixer",
    )(u, vln, pz, pz, pz, ws, bs_full, wp_bd, scale)


def _top2_of4(b, s):
    v1, i1, s1 = b[0], jnp.zeros(b[0].shape, jnp.int32), s[0]
    for j in range(1, 4):
        gt = b[j] > v1
        v1 = jnp.where(gt, b[j], v1)
        i1 = jnp.where(gt, j, i1)
        s1 = jnp.where(gt, s[j], s1)
    v2 = jnp.full(b[0].shape, -jnp.inf, f32)
    i2 = jnp.full(b[0].shape, -1, jnp.int32)
    s2 = jnp.zeros(b[0].shape, f32)
    for j in range(4):
        cand = jnp.where(i1 != j, b[j], -jnp.inf) > v2
        v2 = jnp.where(cand, b[j], v2)
        i2 = jnp.where(cand, j, i2)
        s2 = jnp.where(cand, s[j], s2)
    return v1 + v2, i1, i2, s1, s2


def _outproj_kernel(attn_ref, gm_ref, pool_ref, x_ref, w_ref, g_ref, b_ref, rw_ref, rb_ref, tri_ref,
                    xe_ref, meta_ref, cnt_ref, carry):
    tm = ROW_TILE

    @pl.when(pl.program_id(0) == 0)
    def _():
        carry[...] = jnp.zeros_like(carry)

    o1, o2 = ATTN_WIDTH, ATTN_WIDTH + GMLP_WIDTH
    hmix = jnp.dot(attn_ref[...], w_ref[0:o1, :], preferred_element_type=f32)
    hmix = hmix + jnp.dot(gm_ref[...], w_ref[o1:o2, :], preferred_element_type=f32)
    hmix = hmix + jnp.dot(pool_ref[...], w_ref[o2:, :], preferred_element_type=f32)
    x1 = _layernorm(DEEPNORM_ALPHA * x_ref[...] + hmix, g_ref[...], b_ref[...])
    xe_ref[:, 0:D_MODEL] = x1

    logits = lax.dot_general(rw_ref[...], x1.astype(bf16), (((1,), (1,)), ((), ())),
                             preferred_element_type=f32)
    scores = 1.0 / (1.0 + jnp.exp(-logits))
    biased = scores + rb_ref[...]
    brow = [biased[e:e + 1, :] for e in range(N_EXPERTS)]
    srow = [scores[e:e + 1, :] for e in range(N_EXPERTS)]
    best = None
    for g in range(N_EXPERT_GROUPS):
        sl = slice(g * EXPERTS_PER_GROUP, (g + 1) * EXPERTS_PER_GROUP)
        gs, i1, i2, s1, s2 = _top2_of4(brow[sl], srow[sl])
        if best is None:
            best = (gs, jnp.zeros(gs.shape, jnp.int32), i1, i2, s1, s2)
        else:
            gt = gs > best[0]
            best = (jnp.where(gt, gs, best[0]), jnp.where(gt, g, best[1]), jnp.where(gt, i1, best[2]),
                    jnp.where(gt, i2, best[3]), jnp.where(gt, s1, best[4]), jnp.where(gt, s2, best[5]))
    _, gsel, i1, i2, s1, s2 = best
    den = s1 + s2
    ga, gb = s1 / den, s2 / den
    first_lo = i1 < i2
    lo = jnp.minimum(i1, i2)
    hi = jnp.maximum(i1, i2)
    g_lo = jnp.where(first_lo, ga, gb)
    g_hi = jnp.where(first_lo, gb, ga)
    pidx = jnp.where(lo == 0, hi - 1, jnp.where(lo == 1, hi + 1, N_PAIRS - 1))
    cls = gsel * N_PAIRS + pidx

    onehot = lax.broadcasted_iota(jnp.int32, (CLASS_ROWS, tm), 0) == cls
    ohf = jnp.where(onehot, 1.0, 0.0)
    prefix = jnp.dot(ohf.astype(bf16), tri_ref[...], preferred_element_type=f32)
    base = carry[:, 0:1]
    rank = jnp.sum(ohf * (prefix + base), axis=0, keepdims=True)
    new_carry = carry[...] + jnp.sum(ohf, axis=1, keepdims=True)
    carry[...] = new_carry
    cnt_ref[...] = new_carry

    srow_id = lax.broadcasted_iota(jnp.int32, (8, tm), 0)
    meta_ref[...] = jnp.where(srow_id == 0, cls, jnp.where(srow_id == 1, rank.astype(jnp.int32), 0))
    side8 = jnp.where(srow_id == 0, g_lo, jnp.where(srow_id == 1, g_hi, 0.0))
    side = jnp.concatenate([side8, jnp.zeros((SIDE - 8, tm), f32)], axis=0)
    xe_ref[:, D_MODEL:] = side.T


def _outproj(attn, gm, pool, x, w_out, g, b, rw_pad, rb, tri):
    t = x.shape[0]
    tm = ROW_TILE
    row = lambda w: pl.BlockSpec((tm, w), lambda i: (i, 0))
    const = lambda shape: pl.BlockSpec(shape, lambda i: (0,) * len(shape))
    return pl.pallas_call(
        _outproj_kernel,
        grid=(t // tm,),
        in_specs=[row(ATTN_WIDTH), row(GMLP_WIDTH), row(POOL_WIDTH), row(D_MODEL), const((D_MODEL, D_MODEL)),
                  const((1, D_MODEL)), const((1, D_MODEL)), const((N_EXPERTS, D_MODEL)), const((N_EXPERTS, 1)),
                  const((tm, tm))],
        out_specs=[row(D_MODEL + SIDE), pl.BlockSpec((None, 8, tm), lambda i: (i, 0, 0)),
                   const((CLASS_ROWS, LANES))],
        out_shape=[jax.ShapeDtypeStruct((t, D_MODEL + SIDE), f32), jax.ShapeDtypeStruct((t // tm, 8, tm), jnp.int32),
                   jax.ShapeDtypeStruct((CLASS_ROWS, LANES), f32)],
        scratch_shapes=[pltpu.VMEM((CLASS_ROWS, LANES), f32)],
        compiler_params=pltpu.CompilerParams(dimension_semantics=("arbitrary",), vmem_limit_bytes=VMEM_LIMIT),
        name="outproj_ln_router",
    )(attn, gm, pool, x, w_out, g, b, rw_pad, rb, tri)


def _permute_kernel(idx_ref, zflag_ref, src_ref, *rest, scatter, nzero):
    if scatter:
        dst_ref, zeros, sem, zsem = rest
    else:
        dst_ref, sem = rest
    step = pl.program_id(0)

    if scatter:
        def zcopy(j):
            return pltpu.make_async_copy(zeros, dst_ref.at[pl.ds(pl.multiple_of(j * MOE_BLK, MOE_BLK), MOE_BLK), :],
                                         zsem)

        @pl.when(step == 0)
        def _():
            zeros[...] = jnp.zeros_like(zeros)

            def zstart(j, c):
                @pl.when(zflag_ref[j] != 0)
                def _():
                    zcopy(j).start()
                return c

            def zwait(j, c):
                @pl.when(zflag_ref[j] != 0)
                def _():
                    zcopy(j).wait()
                return c

            lax.fori_loop(0, nzero, zstart, 0)
            lax.fori_loop(0, nzero, zwait, 0)

    def row_copy(r):
        other = idx_ref[0, 0, r]
        if scatter:
            return pltpu.make_async_copy(src_ref.at[pl.ds(r, 1), :], dst_ref.at[pl.ds(other, 1), :], sem)
        return pltpu.make_async_copy(src_ref.at[pl.ds(other, 1), :], dst_ref.at[pl.ds(r, 1), :], sem)

    def start(g, c):
        for k in range(PERM_UNROLL):
            row_copy(g * PERM_UNROLL + k).start(priority=k % 2)
        return c

    lax.fori_loop(0, PERM_CHUNK // PERM_UNROLL, start, 0)
    if scatter:
        pltpu.make_async_copy(src_ref, dst_ref.at[pl.ds(0, PERM_CHUNK), :], sem).wait()
    else:
        pltpu.make_async_copy(src_ref.at[pl.ds(0, PERM_CHUNK), :], dst_ref, sem).wait()


def _permute(idx, zflag, src, n_dst, width, scatter):
    nsteps = idx.shape[0]
    nzero = zflag.shape[0]
    scratch = [pltpu.SemaphoreType.DMA(())]
    if scatter:
        scratch = [pltpu.VMEM((MOE_BLK, width), f32), pltpu.SemaphoreType.DMA(()), pltpu.SemaphoreType.DMA(())]
    block = pl.BlockSpec((PERM_CHUNK, width), lambda i: (i, 0))
    hbm = pl.BlockSpec(memory_space=pl.ANY)
    smem = pl.BlockSpec(memory_space=pltpu.SMEM)
    return pl.pallas_call(
        functools.partial(_permute_kernel, scatter=scatter, nzero=nzero),
        grid=(nsteps,),
        in_specs=[pl.BlockSpec((1, 1, PERM_CHUNK), lambda i: (i, 0, 0), memory_space=pltpu.SMEM), smem,
                  block if scatter else hbm],
        out_specs=hbm if scatter else block,
        out_shape=jax.ShapeDtypeStruct((n_dst, width), f32),
        scratch_shapes=scratch,
        compiler_params=pltpu.CompilerParams(dimension_semantics=("arbitrary",), vmem_limit_bytes=VMEM_LIMIT),
        name="dispatch_rows" if scatter else "undispatch_rows",
    )(idx, zflag, src)


def _expert_kernel(elo_ref, ehi_ref, nused_ref, new_ref, xb_ref, wg_lo, wu_lo, wd_lo, wg_hi, wu_hi, wd_hi,
                   g_ref, b_ref, out_ref, *scratch):
    wb, pre = scratch[:6], scratch[6]
    j = pl.program_id(0)
    last = pl.num_programs(0) - 1
    jc = jnp.minimum(j, last - 1)

    @pl.when(j == 0)
    def _():
        pre[...] = jnp.zeros_like(pre)

    @pl.when((new_ref[jc] != 0) & (j < last))
    def _():
        for src, dst in zip((wg_lo, wu_lo, wd_lo, wg_hi, wu_hi, wd_hi), wb):
            dst[...] = src[...].astype(bf16)

    @pl.when(j <= nused_ref[0])
    def _():
        out_ref[...] = _layernorm(pre[(j + 1) & 1], g_ref[...], b_ref[...])
        x = xb_ref[:, 0:D_MODEL]
        gates = xb_ref[:, D_MODEL:D_MODEL + SIDE]
        xb = x.astype(bf16)
        moe = jnp.zeros_like(x)
        for col, (wg, wu, wd) in enumerate((wb[0:3], wb[3:6])):
            a = jnp.dot(xb, wg[...], preferred_element_type=f32)
            up = jnp.dot(xb, wu[...], preferred_element_type=f32)
            hidden = (a / (1.0 + jnp.exp(-a))) * up
            y = jnp.dot(hidden.astype(bf16), wd[...], preferred_element_type=f32)
            moe = moe + y * gates[:, col:col + 1]
        pre[j & 1] = DEEPNORM_ALPHA * x + moe

    @pl.when(j > nused_ref[0])
    def _():
        out_ref[...] = jnp.zeros_like(out_ref)


def _experts(elo, ehi, nused, newcls, buf, wg, wu, wd, g, b, layer):
    p = buf.shape[0]
    nb = p // MOE_BLK
    cur = lambda j: jnp.minimum(j, nb - 1)
    wspec = lambda shape, which: pl.BlockSpec(
        (None, None) + shape, (lambda j, elo, ehi, nu, nw: (layer, elo[cur(j)], 0, 0)) if which == 0
        else (lambda j, elo, ehi, nu, nw: (layer, ehi[cur(j)], 0, 0)))
    up_shape, down_shape = (D_MODEL, EXPERT_FF), (EXPERT_FF, D_MODEL)
    const = lambda shape: pl.BlockSpec(shape, lambda j, elo, ehi, nu, nw: (0,) * len(shape))
    grid_spec = pltpu.PrefetchScalarGridSpec(
        num_scalar_prefetch=4,
        grid=(nb + 1,),
        in_specs=[pl.BlockSpec((MOE_BLK, D_MODEL + SIDE), lambda j, elo, ehi, nu, nw: (cur(j), 0)),
                  wspec(up_shape, 0), wspec(up_shape, 0), wspec(down_shape, 0),
                  wspec(up_shape, 1), wspec(up_shape, 1), wspec(down_shape, 1),
                  const((1, D_MODEL)), const((1, D_MODEL))],
        out_specs=pl.BlockSpec((MOE_BLK, D_MODEL), lambda j, elo, ehi, nu, nw: (jnp.maximum(j - 1, 0), 0)),
        scratch_shapes=[pltpu.VMEM(up_shape, bf16), pltpu.VMEM(up_shape, bf16), pltpu.VMEM(down_shape, bf16)] * 2
        + [pltpu.VMEM((2, MOE_BLK, D_MODEL), f32)],
    )
    return pl.pallas_call(
        _expert_kernel,
        grid_spec=grid_spec,
        out_shape=jax.ShapeDtypeStruct((p, D_MODEL), f32),
        compiler_params=pltpu.CompilerParams(dimension_semantics=("arbitrary",), vmem_limit_bytes=VMEM_LIMIT),
        name="expert_pairs",
    )(elo, ehi, nused, newcls, buf, wg, wu, wd, wg, wu, wd, g, b)


def _rope_tables(seq):
    half = ROPE_DIM // 2
    inv_freq = jnp.power(jnp.float32(ROPE_THETA), -jnp.arange(half, dtype=f32) / half)
    ang = jnp.arange(seq, dtype=f32)[:, None] * inv_freq[None, :]
    lane = np.arange(LANES) % HEAD_DIM
    fidx = lane % half
    in_rope = lane < ROPE_DIM
    first = lane < half
    cos_l = jnp.cos(ang)[:, fidx]
    sin_l = jnp.sin(ang)[:, fidx]
    cos_t = jnp.where(in_rope[None, :], cos_l, 1.0)
    sa_t = jnp.where((in_rope & first)[None, :], -sin_l, 0.0)
    sb_t = jnp.where((in_rope & ~first)[None, :], sin_l, 0.0)
    return cos_t, sa_t, sb_t


_PAIRS = [(a, b) for a in range(EXPERTS_PER_GROUP) for b in range(a + 1, EXPERTS_PER_GROUP)]


def _dispatch_plan(counts, meta, t):
    counts = counts.astype(jnp.int32)
    padded = ((counts + MOE_BLK - 1) // MOE_BLK) * MOE_BLK
    pend = jnp.cumsum(padded)
    pstart = pend - padded
    nb = (t + N_CLASSES * MOE_BLK) // MOE_BLK
    blk_start = jnp.arange(nb, dtype=jnp.int32) * MOE_BLK
    blk_cls = jnp.minimum(jnp.sum((blk_start[:, None] >= pend[None, :]).astype(jnp.int32), axis=1), N_CLASSES - 1)
    valid_end = pstart[blk_cls] + counts[blk_cls]
    zflag = ((blk_start + MOE_BLK > valid_end) | (blk_start >= pend[-1])).astype(jnp.int32)
    pair_lo = jnp.asarray([p[0] for p in _PAIRS], jnp.int32)
    pair_hi = jnp.asarray([p[1] for p in _PAIRS], jnp.int32)
    grp = blk_cls // N_PAIRS
    elo = grp * EXPERTS_PER_GROUP + pair_lo[blk_cls % N_PAIRS]
    ehi = grp * EXPERTS_PER_GROUP + pair_hi[blk_cls % N_PAIRS]
    nused = (pend[-1] // MOE_BLK).astype(jnp.int32).reshape(1)
    newcls = jnp.concatenate([jnp.ones((1,), jnp.int32), (blk_cls[1:] != blk_cls[:-1]).astype(jnp.int32)])
    cls, rank = meta[:, 0:1, :], meta[:, 1:2, :]
    dest = rank
    for c in range(N_CLASSES):
        dest = dest + jnp.where(cls == c, pstart[c], 0)
    return dest, zflag, elo, ehi, nused, newcls


def kernel(x, w_in, w_out, gmlp_ln_g, gmlp_ln_b, gmlp_w_s, gmlp_b_s, pool_w, pool_scale, ln1_g, ln1_b,
           router_w, router_bias, w_gate, w_up, w_down, ln2_g, ln2_b):
    batch, seq, d = x.shape
    t = batch * seq
    assert d == D_MODEL and seq % ROW_TILE == 0 and t % PERM_CHUNK == 0
    cos_t, sa_t, sb_t = _rope_tables(seq)
    tri = (np.arange(ROW_TILE)[:, None] < np.arange(ROW_TILE)[None, :])
    tri = jnp.asarray(tri, bf16)
    rw_pad = router_w.T.astype(bf16)
    rb = router_bias.reshape(N_EXPERTS, 1)
    n_rows = t + N_CLASSES * MOE_BLK
    dummy_flag = jnp.zeros((1,), jnp.int32)

    xf = x.reshape(t, d)
    for l in range(DEPTH):
        lng = gmlp_ln_g[l].reshape(1, GMLP_WIDTH)
        lnb = gmlp_ln_b[l].reshape(1, GMLP_WIDTH)
        qp, kp, vp, u, vln, pz = _inproj(xf, w_in[l].astype(bf16), cos_t, sa_t, sb_t, lng, lnb, seq)
        attn = _attention(qp, kp, vp, batch, seq)
        bs_full = jnp.repeat(gmlp_b_s[l].T, HEAD_DIM, axis=1)
        wp_bd = jnp.zeros((POOL_WIDTH, POOL_WIDTH), f32)
        for g in range(len(POOL_WINDOWS)):
            wp_bd = wp_bd.at[g * HEAD_DIM:(g + 1) * HEAD_DIM, g * HEAD_DIM:(g + 1) * HEAD_DIM].set(pool_w[l, g])
        wp_bd = wp_bd.astype(bf16)
        gm, pool = _mixer(u, vln, pz, gmlp_w_s[l].astype(bf16), bs_full, wp_bd,
                          pool_scale[l].reshape(1, POOL_WIDTH), seq)
        xe, meta, counts = _outproj(attn, gm, pool, xf, w_out[l].astype(bf16), ln1_g[l].reshape(1, d),
                                    ln1_b[l].reshape(1, d), rw_pad, rb, tri)
        dest, zflag, elo, ehi, nused, newcls = _dispatch_plan(counts[:N_CLASSES, 0], meta, t)
        dest = dest.reshape(t // PERM_CHUNK, 1, PERM_CHUNK)
        buf = _permute(dest, zflag, xe, n_rows, D_MODEL + SIDE, scatter=True)
        ys = _experts(elo, ehi, nused, newcls, buf, w_gate, w_up, w_down,
                      ln2_g[l].reshape(1, d), ln2_b[l].reshape(1, d), l)
        xf = _permute(dest, dummy_flag, ys, t, D_MODEL, scatter=False)
    return xf.reshape(batch, seq, d)
```

```python
import functools

import jax
import jax.numpy as jnp
import numpy as np
from jax import lax
from jax.experimental import pallas as pl
from jax.experimental.pallas import tpu as pltpu

f32 = jnp.float32
bf16 = jnp.bfloat16

D_MODEL = 1024
HEAD_DIM = 64
ATTN_WIDTH = 512
DILATED_CONFIGS = ((128, 1), (512, 4), (2048, 16))
ATTN_HALF = 64
ROPE_THETA = 500000.0
ROPE_DIM = 16
GMLP_WIDTH = 256
GMLP_GROUPS = 4
GMLP_CHUNK = 128
POOL_WIDTH = 256
POOL_WINDOWS = (2, 4, 8, 16)
IN_COLS = 3 * ATTN_WIDTH + 2 * GMLP_WIDTH + POOL_WIDTH
N_EXPERTS = 16
N_EXPERT_GROUPS = 4
EXPERTS_PER_GROUP = 4
EXPERT_FF = 512
DEPTH = 2
DEEPNORM_ALPHA = float((2 * DEPTH) ** 0.25)
LN_EPS = 1e-5
NEG_INF = -1e30

LANES = 128
ROW_TILE = 1024
ATTN_QBLK = 128
POOL_HALO = 32
N_PAIRS = 6
N_CLASSES = N_EXPERT_GROUPS * N_PAIRS
CLASS_ROWS = 32
MOE_BLK = 256
SIDE = LANES
PERM_CHUNK = 2048
PERM_UNROLL = 16
VMEM_LIMIT = 48 * 1024 * 1024


def _layernorm(y, g, b):
    mu = jnp.mean(y, axis=-1, keepdims=True)
    d = y - mu
    var = jnp.mean(d * d, axis=-1, keepdims=True)
    return d * lax.rsqrt(var + LN_EPS) * g + b


def _group_mean(v):
    lane = lax.broadcasted_iota(jnp.int32, v.shape, 1)
    out = jnp.zeros_like(v)
    for g in range(GMLP_GROUPS):
        m = (lane >= g * HEAD_DIM) & (lane < (g + 1) * HEAD_DIM)
        s = jnp.sum(jnp.where(m, v, 0.0), axis=-1, keepdims=True)
        out = jnp.where(m, s, out)
    return out * (1.0 / HEAD_DIM)


def _inproj_kernel(x_ref, w_ref, cos_ref, sa_ref, sb_ref, lng_ref, lnb_ref,
                   q_ref, k_ref, v_ref, u_ref, vln_ref, pz_ref):
    xb = x_ref[...].astype(bf16)
    cos = cos_ref[...]
    sa = sa_ref[...]
    sb = sb_ref[...]
    for base, out_ref, rope, scale in ((0, q_ref, True, HEAD_DIM ** -0.5),
                                       (ATTN_WIDTH, k_ref, True, 1.0),
                                       (2 * ATTN_WIDTH, v_ref, False, 1.0)):
        y = jnp.dot(xb, w_ref[:, base:base + ATTN_WIDTH], preferred_element_type=f32)
        for p in range(ATTN_WIDTH // LANES):
            yp = y[:, p * LANES:(p + 1) * LANES]
            if rope:
                yp = yp * cos + pltpu.roll(yp, LANES - ROPE_DIM // 2, 1) * sa + pltpu.roll(yp, ROPE_DIM // 2, 1) * sb
            if scale != 1.0:
                yp = yp * scale
            out_ref[p] = yp.astype(bf16)
    o3 = 3 * ATTN_WIDTH
    y = jnp.dot(xb, w_ref[:, o3:o3 + 3 * GMLP_WIDTH], preferred_element_type=f32)
    u_ref[...] = jax.nn.gelu(y[:, :GMLP_WIDTH])
    gv = jax.nn.gelu(y[:, GMLP_WIDTH:2 * GMLP_WIDTH])
    mu = _group_mean(gv)
    d = gv - mu
    var = _group_mean(d * d)
    vln_ref[...] = (d * lax.rsqrt(var + LN_EPS) * lng_ref[...] + lnb_ref[...]).astype(bf16)
    pz_ref[...] = y[:, 2 * GMLP_WIDTH:]


def _inproj(x, w_in, cos_t, sa_t, sb_t, lng, lnb, seq):
    t = x.shape[0]
    nt = t // ROW_TILE
    tiles_per_seq = seq // ROW_TILE
    npair = ATTN_WIDTH // LANES
    tab_spec = pl.BlockSpec((ROW_TILE, LANES), lambda i: (i % tiles_per_seq, 0))
    row = lambda w: pl.BlockSpec((ROW_TILE, w), lambda i: (i, 0))
    const = lambda shape: pl.BlockSpec(shape, lambda i: (0,) * len(shape))
    pair_spec = pl.BlockSpec((npair, ROW_TILE, LANES), lambda i: (0, i, 0))
    pair_shape = jax.ShapeDtypeStruct((npair, t, LANES), bf16)
    return pl.pallas_call(
        _inproj_kernel,
        grid=(nt,),
        in_specs=[row(D_MODEL), const((D_MODEL, IN_COLS)), tab_spec, tab_spec, tab_spec,
                  const((1, GMLP_WIDTH)), const((1, GMLP_WIDTH))],
        out_specs=[pair_spec, pair_spec, pair_spec, row(GMLP_WIDTH), row(GMLP_WIDTH), row(POOL_WIDTH)],
        out_shape=[pair_shape, pair_shape, pair_shape,
                   jax.ShapeDtypeStruct((t, GMLP_WIDTH), f32),
                   jax.ShapeDtypeStruct((t, GMLP_WIDTH), bf16),
                   jax.ShapeDtypeStruct((t, POOL_WIDTH), f32)],
        compiler_params=pltpu.CompilerParams(dimension_semantics=("parallel",), vmem_limit_bytes=VMEM_LIMIT),
        name="inproj",
    )(x, w_in, cos_t, sa_t, sb_t, lng, lnb)


def _attn_kernel(q_ref, k_ref, v_ref, bias_ref, o_ref, qs, ks, vs, o0, o1, o2, l0s, l1s, l2s, *, seq):
    os_ = (o0, o1, o2)
    ls_ = (l0s, l1s, l2s)
    qs[...] = q_ref[...].astype(f32)
    ks[...] = k_ref[...].astype(f32)
    vs[...] = v_ref[...].astype(f32)
    lane = lax.broadcasted_iota(jnp.int32, (ATTN_QBLK, LANES), 1)
    head_a = lane < HEAD_DIM

    for ci, (_, dil) in enumerate(DILATED_CONFIGS):
        length = seq // dil
        nblk = length // ATTN_QBLK
        win = min(2 * ATTN_QBLK, length)
        shift = dil.bit_length() - 1

        def rows(start, size, dil=dil):
            if dil == 1:
                return pl.ds(pl.multiple_of(start, 8), size)
            return pl.ds(start, size, stride=dil)

        def body(u, carry, ci=ci, dil=dil, length=length, win=win, shift=shift, rows=rows):
            c = u & (dil - 1)
            i = u >> shift
            k0 = jnp.clip(i * ATTN_QBLK - ATTN_HALF, 0, length - win)
            delta = k0 - i * ATTN_QBLK
            qrows = rows(c + dil * ATTN_QBLK * i, ATTN_QBLK)
            krows = rows(c + dil * k0, win)
            q = qs[qrows, :]
            kb = ks[krows, :].astype(bf16)
            vb = vs[krows, :].astype(bf16)
            bias = bias_ref[lax.shift_right_logical(-delta, ATTN_HALF.bit_length() - 1), :, 0:win]
            q2 = jnp.concatenate([jnp.where(head_a, q, 0.0), jnp.where(head_a, 0.0, q)], axis=0).astype(bf16)
            s = lax.dot_general(q2, kb, (((1,), (1,)), ((), ())), preferred_element_type=f32) + bias
            m = jnp.max(s, axis=-1, keepdims=True)
            p = jnp.exp(s - m).astype(bf16)
            ol = jnp.dot(p, jnp.concatenate([vb, jnp.ones_like(vb)], axis=1), preferred_element_type=f32)
            l = ol[:, LANES:]
            o = ol[:, :LANES] / l
            lse = m + jnp.log(l)
            os_[ci][qrows, :] = jnp.where(head_a, o[:ATTN_QBLK], o[ATTN_QBLK:])
            ls_[ci][qrows, :] = jnp.where(head_a, lse[:ATTN_QBLK], lse[ATTN_QBLK:])
            return carry

        lax.fori_loop(0, dil * nblk, body, 0, unroll=8)

    chunk = 256

    def combine(j, carry):
        r = pl.ds(pl.multiple_of(j * chunk, chunk), chunk)
        l0, l1, l2 = l0s[r, :], l1s[r, :], l2s[r, :]
        mx = jnp.maximum(jnp.maximum(l0, l1), l2)
        w0, w1, w2 = jnp.exp(l0 - mx), jnp.exp(l1 - mx), jnp.exp(l2 - mx)
        num = w0 * o0[r, :] + w1 * o1[r, :] + w2 * o2[r, :]
        o_ref[r, :] = (num / (w0 + w1 + w2)).astype(bf16)
        return carry

    lax.fori_loop(0, seq // chunk, combine, 0)


def _attention(qp, kp, vp, batch, seq):
    npair = qp.shape[0]
    t = batch * seq
    in_spec = pl.BlockSpec((None, seq, LANES), lambda p, b: (p, b, 0))
    win = 2 * ATTN_QBLK
    rr = np.arange(2 * ATTN_QBLK)[None, :, None] % ATTN_QBLK
    cc = np.arange(win)[None, None, :]
    dd = np.arange(3)[:, None, None] * ATTN_HALF
    bias = jnp.asarray(np.where(np.abs(cc - dd - rr) <= ATTN_HALF, 0.0, NEG_INF), f32)
    return pl.pallas_call(
        functools.partial(_attn_kernel, seq=seq),
        grid=(npair, batch),
        in_specs=[in_spec, in_spec, in_spec, pl.BlockSpec(bias.shape, lambda p, b: (0, 0, 0))],
        out_specs=pl.BlockSpec((seq, LANES), lambda p, b: (b, p)),
        out_shape=jax.ShapeDtypeStruct((t, ATTN_WIDTH), bf16),
        scratch_shapes=[pltpu.VMEM((seq, LANES), f32)] * (3 + 2 * len(DILATED_CONFIGS)),
        compiler_params=pltpu.CompilerParams(dimension_semantics=("parallel", "parallel"),
                                             vmem_limit_bytes=VMEM_LIMIT),
        name="dilated_attn",
    )(qp, kp, vp, bias)


def _mixer_kernel(u_ref, vln_ref, pz_ref, prev_ref, next_ref, ws_ref, bs_ref, wp_ref, sc_ref,
                  gm_ref, pool_ref, z0, z1, z2, z3, z4, *, seq):
    tm = ROW_TILE
    h = POOL_HALO
    lane = lax.broadcasted_iota(jnp.int32, (GMLP_CHUNK, GMLP_WIDTH), 1)
    for cc in range(tm // GMLP_CHUNK):
        r = slice(cc * GMLP_CHUNK, (cc + 1) * GMLP_CHUNK)
        vc = vln_ref[r, :]
        sv = jnp.zeros((GMLP_CHUNK, GMLP_WIDTH), f32)
        for g in range(GMLP_GROUPS):
            svg = jnp.dot(ws_ref[g], vc, preferred_element_type=f32)
            sv = jnp.where((lane >= g * HEAD_DIM) & (lane < (g + 1) * HEAD_DIM), svg, sv)
        gm_ref[r, :] = (u_ref[r, :] * (sv + bs_ref[...])).astype(bf16)

    i = pl.program_id(0)
    tiles_per_seq = seq // tm
    pos0 = (i % tiles_per_seq) * tm
    first = pos0 == 0
    last = pos0 + tm == seq
    z0[0:h, :] = jnp.where(first, 0.0, prev_ref[...])
    z0[h:h + tm, :] = pz_ref[...]
    z0[h + tm:h + tm + h, :] = jnp.where(last, 0.0, next_ref[...])
    z1[8:tm + 2 * h - 8, :] = z0[8:tm + 2 * h - 8, :] + z0[7:tm + 2 * h - 9, :]
    z2[16:tm + 2 * h - 16, :] = z1[15:tm + 2 * h - 17, :] + z1[17:tm + 2 * h - 15, :]
    z3[24:tm + 2 * h - 24, :] = z2[22:tm + 2 * h - 26, :] + z2[26:tm + 2 * h - 22, :]
    z4[h:h + tm, :] = z3[h - 4:h + tm - 4, :] + z3[h + 4:h + tm + 4, :]
    lane_t = lax.broadcasted_iota(jnp.int32, (tm, POOL_WIDTH), 1)
    pos = pos0 + lax.broadcasted_iota(jnp.int32, (tm, POOL_WIDTH), 0)
    zc = z0[h:h + tm, :]
    pooled = jnp.zeros((tm, POOL_WIDTH), f32)
    for g, (w, zw) in enumerate(zip(POOL_WINDOWS, (z1, z2, z3, z4))):
        left = w // 2
        right = w - 1 - left
        lo = jnp.maximum(pos - left, 0)
        hi = jnp.minimum(pos + right + 1, seq)
        cnt = (hi - lo).astype(f32)
        val = zw[h:h + tm, :] / cnt - zc
        pooled = jnp.where((lane_t >= g * HEAD_DIM) & (lane_t < (g + 1) * HEAD_DIM), val, pooled)
    y = jnp.dot(pooled.astype(bf16), wp_ref[...], preferred_element_type=f32)
    pool_ref[...] = (y * sc_ref[...]).astype(bf16)


def _mixer(u, vln, pz, ws, bs_full, wp_bd, scale, seq):
    t = u.shape[0]
    tm, h = ROW_TILE, POOL_HALO
    nt = t // tm
    hb = tm // h
    row = lambda w: pl.BlockSpec((tm, w), lambda i: (i, 0))
    const = lambda shape: pl.BlockSpec(shape, lambda i: (0,) * len(shape))
    prev_spec = pl.BlockSpec((h, POOL_WIDTH), lambda i: (jnp.maximum(i * hb - 1, 0), 0))
    next_spec = pl.BlockSpec((h, POOL_WIDTH), lambda i: (jnp.minimum((i + 1) * hb, t // h - 1), 0))
    zbuf = pltpu.VMEM((tm + 2 * h, POOL_WIDTH), f32)
    return pl.pallas_call(
        functools.partial(_mixer_kernel, seq=seq),
        grid=(nt,),
        in_specs=[row(GMLP_WIDTH), row(GMLP_WIDTH), row(POOL_WIDTH), prev_spec, next_spec,
                  const((GMLP_GROUPS, GMLP_CHUNK, GMLP_CHUNK)), const((GMLP_CHUNK, GMLP_WIDTH)),
                  const((POOL_WIDTH, POOL_WIDTH)), const((1, POOL_WIDTH))],
        out_specs=[row(GMLP_WIDTH), row(POOL_WIDTH)],
        out_shape=[jax.ShapeDtypeStruct((t, GMLP_WIDTH), bf16), jax.ShapeDtypeStruct((t, POOL_WIDTH), bf16)],
        scratch_shapes=[zbuf] * 5,
        compiler_params=pltpu.CompilerParams(dimension_semantics=("parallel",), vmem_limit_bytes=VMEM_LIMIT),
        name="mixer",
    )(u, vln, pz, pz, pz, ws, bs_full, wp_bd, scale)


def _top2_of4(b, s):
    v1, i1, s1 = b[0], jnp.zeros(b[0].shape, jnp.int32), s[0]
    for j in range(1, 4):
        gt = b[j] > v1
        v1 = jnp.where(gt, b[j], v1)
        i1 = jnp.where(gt, j, i1)
        s1 = jnp.where(gt, s[j], s1)
    v2 = jnp.full(b[0].shape, -jnp.inf, f32)
    i2 = jnp.full(b[0].shape, -1, jnp.int32)
    s2 = jnp.zeros(b[0].shape, f32)
    for j in range(4):
        cand = jnp.where(i1 != j, b[j], -jnp.inf) > v2
        v2 = jnp.where(cand, b[j], v2)
        i2 = jnp.where(cand, j, i2)
        s2 = jnp.where(cand, s[j], s2)
    return v1 + v2, i1, i2, s1, s2


def _outproj_kernel(attn_ref, gm_ref, pool_ref, x_ref, w_ref, g_ref, b_ref, rw_ref, rb_ref, tri_ref,
                    xe_ref, meta_ref, cnt_ref, carry):
    tm = ROW_TILE

    @pl.when(pl.program_id(0) == 0)
    def _():
        carry[...] = jnp.zeros_like(carry)

    o1, o2 = ATTN_WIDTH, ATTN_WIDTH + GMLP_WIDTH
    hmix = jnp.dot(attn_ref[...], w_ref[0:o1, :], preferred_element_type=f32)
    hmix = hmix + jnp.dot(gm_ref[...], w_ref[o1:o2, :], preferred_element_type=f32)
    hmix = hmix + jnp.dot(pool_ref[...], w_ref[o2:, :], preferred_element_type=f32)
    x1 = _layernorm(DEEPNORM_ALPHA * x_ref[...] + hmix, g_ref[...], b_ref[...])
    xe_ref[:, 0:D_MODEL] = x1

    logits = lax.dot_general(rw_ref[...], x1.astype(bf16), (((1,), (1,)), ((), ())),
                             preferred_element_type=f32)
    scores = 1.0 / (1.0 + jnp.exp(-logits))
    biased = scores + rb_ref[...]
    brow = [biased[e:e + 1, :] for e in range(N_EXPERTS)]
    srow = [scores[e:e + 1, :] for e in range(N_EXPERTS)]
    best = None
    for g in range(N_EXPERT_GROUPS):
        sl = slice(g * EXPERTS_PER_GROUP, (g + 1) * EXPERTS_PER_GROUP)
        gs, i1, i2, s1, s2 = _top2_of4(brow[sl], srow[sl])
        if best is None:
            best = (gs, jnp.zeros(gs.shape, jnp.int32), i1, i2, s1, s2)
        else:
            gt = gs > best[0]
            best = (jnp.where(gt, gs, best[0]), jnp.where(gt, g, best[1]), jnp.where(gt, i1, best[2]),
                    jnp.where(gt, i2, best[3]), jnp.where(gt, s1, best[4]), jnp.where(gt, s2, best[5]))
    _, gsel, i1, i2, s1, s2 = best
    den = s1 + s2
    ga, gb = s1 / den, s2 / den
    first_lo = i1 < i2
    lo = jnp.minimum(i1, i2)
    hi = jnp.maximum(i1, i2)
    g_lo = jnp.where(first_lo, ga, gb)
    g_hi = jnp.where(first_lo, gb, ga)
    pidx = jnp.where(lo == 0, hi - 1, jnp.where(lo == 1, hi + 1, N_PAIRS - 1))
    cls = gsel * N_PAIRS + pidx

    onehot = lax.broadcasted_iota(jnp.int32, (CLASS_ROWS, tm), 0) == cls
    ohf = jnp.where(onehot, 1.0, 0.0)
    prefix = jnp.dot(ohf.astype(bf16), tri_ref[...], preferred_element_type=f32)
    base = carry[:, 0:1]
    rank = jnp.sum(ohf * (prefix + base), axis=0, keepdims=True)
    new_carry = carry[...] + jnp.sum(ohf, axis=1, keepdims=True)
    carry[...] = new_carry
    cnt_ref[...] = new_carry

    srow_id = lax.broadcasted_iota(jnp.int32, (8, tm), 0)
    meta_ref[...] = jnp.where(srow_id == 0, cls, jnp.where(srow_id == 1, rank.astype(jnp.int32), 0))
    side8 = jnp.where(srow_id == 0, g_lo, jnp.where(srow_id == 1, g_hi, 0.0))
    side = jnp.concatenate([side8, jnp.zeros((SIDE - 8, tm), f32)], axis=0)
    xe_ref[:, D_MODEL:] = side.T


def _outproj(attn, gm, pool, x, w_out, g, b, rw_pad, rb, tri):
    t = x.shape[0]
    tm = ROW_TILE
    row = lambda w: pl.BlockSpec((tm, w), lambda i: (i, 0))
    const = lambda shape: pl.BlockSpec(shape, lambda i: (0,) * len(shape))
    return pl.pallas_call(
        _outproj_kernel,
        grid=(t // tm,),
        in_specs=[row(ATTN_WIDTH), row(GMLP_WIDTH), row(POOL_WIDTH), row(D_MODEL), const((D_MODEL, D_MODEL)),
                  const((1, D_MODEL)), const((1, D_MODEL)), const((N_EXPERTS, D_MODEL)), const((N_EXPERTS, 1)),
                  const((tm, tm))],
        out_specs=[row(D_MODEL + SIDE), pl.BlockSpec((None, 8, tm), lambda i: (i, 0, 0)),
                   const((CLASS_ROWS, LANES))],
        out_shape=[jax.ShapeDtypeStruct((t, D_MODEL + SIDE), f32), jax.ShapeDtypeStruct((t // tm, 8, tm), jnp.int32),
                   jax.ShapeDtypeStruct((CLASS_ROWS, LANES), f32)],
        scratch_shapes=[pltpu.VMEM((CLASS_ROWS, LANES), f32)],
        compiler_params=pltpu.CompilerParams(dimension_semantics=("arbitrary",), vmem_limit_bytes=VMEM_LIMIT),
        name="outproj_ln_router",
    )(attn, gm, pool, x, w_out, g, b, rw_pad, rb, tri)


def _permute_kernel(idx_ref, zflag_ref, src_ref, *rest, scatter, nzero):
    if scatter:
        dst_ref, zeros, sem, zsem = rest
    else:
        dst_ref, sem = rest
    step = pl.program_id(0)

    if scatter:
        def zcopy(j):
            return pltpu.make_async_copy(zeros, dst_ref.at[pl.ds(pl.multiple_of(j * MOE_BLK, MOE_BLK), MOE_BLK), :],
                                         zsem)

        @pl.when(step == 0)
        def _():
            zeros[...] = jnp.zeros_like(zeros)

            def zstart(j, c):
                @pl.when(zflag_ref[j] != 0)
                def _():
                    zcopy(j).start()
                return c

            def zwait(j, c):
                @pl.when(zflag_ref[j] != 0)
                def _():
                    zcopy(j).wait()
                return c

            lax.fori_loop(0, nzero, zstart, 0)
            lax.fori_loop(0, nzero, zwait, 0)

    def row_copy(r):
        other = idx_ref[0, 0, r]
        if scatter:
            return pltpu.make_async_copy(src_ref.at[pl.ds(r, 1), :], dst_ref.at[pl.ds(other, 1), :], sem)
        return pltpu.make_async_copy(src_ref.at[pl.ds(other, 1), :], dst_ref.at[pl.ds(r, 1), :], sem)

    def start(g, c):
        for k in range(PERM_UNROLL):
            row_copy(g * PERM_UNROLL + k).start(priority=k % 2)
        return c

    lax.fori_loop(0, PERM_CHUNK // PERM_UNROLL, start, 0)
    if scatter:
        pltpu.make_async_copy(src_ref, dst_ref.at[pl.ds(0, PERM_CHUNK), :], sem).wait()
    else:
        pltpu.make_async_copy(src_ref.at[pl.ds(0, PERM_CHUNK), :], dst_ref, sem).wait()


def _permute(idx, zflag, src, n_dst, width, scatter):
    nsteps = idx.shape[0]
    nzero = zflag.shape[0]
    scratch = [pltpu.SemaphoreType.DMA(())]
    if scatter:
        scratch = [pltpu.VMEM((MOE_BLK, width), f32), pltpu.SemaphoreType.DMA(()), pltpu.SemaphoreType.DMA(())]
    block = pl.BlockSpec((PERM_CHUNK, width), lambda i: (i, 0))
    hbm = pl.BlockSpec(memory_space=pl.ANY)
    smem = pl.BlockSpec(memory_space=pltpu.SMEM)
    return pl.pallas_call(
        functools.partial(_permute_kernel, scatter=scatter, nzero=nzero),
        grid=(nsteps,),
        in_specs=[pl.BlockSpec((1, 1, PERM_CHUNK), lambda i: (i, 0, 0), memory_space=pltpu.SMEM), smem,
                  block if scatter else hbm],
        out_specs=hbm if scatter else block,
        out_shape=jax.ShapeDtypeStruct((n_dst, width), f32),
        scratch_shapes=scratch,
        compiler_params=pltpu.CompilerParams(dimension_semantics=("arbitrary",), vmem_limit_bytes=VMEM_LIMIT),
        name="dispatch_rows" if scatter else "undispatch_rows",
    )(idx, zflag, src)


def _expert_kernel(elo_ref, ehi_ref, nused_ref, new_ref, xb_ref, wg_lo, wu_lo, wd_lo, wg_hi, wu_hi, wd_hi,
                   g_ref, b_ref, out_ref, *scratch):
    wb, pre = scratch[:6], scratch[6]
    j = pl.program_id(0)
    last = pl.num_programs(0) - 1
    jc = jnp.minimum(j, last - 1)

    @pl.when(j == 0)
    def _():
        pre[...] = jnp.zeros_like(pre)

    @pl.when((new_ref[jc] != 0) & (j < last))
    def _():
        for src, dst in zip((wg_lo, wu_lo, wd_lo, wg_hi, wu_hi, wd_hi), wb):
            dst[...] = src[...].astype(bf16)

    @pl.when(j <= nused_ref[0])
    def _():
        out_ref[...] = _layernorm(pre[(j + 1) & 1], g_ref[...], b_ref[...])
        x = xb_ref[:, 0:D_MODEL]
        gates = xb_ref[:, D_MODEL:D_MODEL + SIDE]
        xb = x.astype(bf16)
        moe = jnp.zeros_like(x)
        for col, (wg, wu, wd) in enumerate((wb[0:3], wb[3:6])):
            a = jnp.dot(xb, wg[...], preferred_element_type=f32)
            up = jnp.dot(xb, wu[...], preferred_element_type=f32)
            hidden = (a / (1.0 + jnp.exp(-a))) * up
            y = jnp.dot(hidden.astype(bf16), wd[...], preferred_element_type=f32)
            moe = moe + y * gates[:, col:col + 1]
        pre[j & 1] = DEEPNORM_ALPHA * x + moe

    @pl.when(j > nused_ref[0])
    def _():
        out_ref[...] = jnp.zeros_like(out_ref)


def _experts(elo, ehi, nused, newcls, buf, wg, wu, wd, g, b, layer):
    p = buf.shape[0]
    nb = p // MOE_BLK
    cur = lambda j: jnp.minimum(j, nb - 1)
    wspec = lambda shape, which: pl.BlockSpec(
        (None, None) + shape, (lambda j, elo, ehi, nu, nw: (layer, elo[cur(j)], 0, 0)) if which == 0
        else (lambda j, elo, ehi, nu, nw: (layer, ehi[cur(j)], 0, 0)))
    up_shape, down_shape = (D_MODEL, EXPERT_FF), (EXPERT_FF, D_MODEL)
    const = lambda shape: pl.BlockSpec(shape, lambda j, elo, ehi, nu, nw: (0,) * len(shape))
    grid_spec = pltpu.PrefetchScalarGridSpec(
        num_scalar_prefetch=4,
        grid=(nb + 1,),
        in_specs=[pl.BlockSpec((MOE_BLK, D_MODEL + SIDE), lambda j, elo, ehi, nu, nw: (cur(j), 0)),
                  wspec(up_shape, 0), wspec(up_shape, 0), wspec(down_shape, 0),
                  wspec(up_shape, 1), wspec(up_shape, 1), wspec(down_shape, 1),
                  const((1, D_MODEL)), const((1, D_MODEL))],
        out_specs=pl.BlockSpec((MOE_BLK, D_MODEL), lambda j, elo, ehi, nu, nw: (jnp.maximum(j - 1, 0), 0)),
        scratch_shapes=[pltpu.VMEM(up_shape, bf16), pltpu.VMEM(up_shape, bf16), pltpu.VMEM(down_shape, bf16)] * 2
        + [pltpu.VMEM((2, MOE_BLK, D_MODEL), f32)],
    )
    return pl.pallas_call(
        _expert_kernel,
        grid_spec=grid_spec,
        out_shape=jax.ShapeDtypeStruct((p, D_MODEL), f32),
        compiler_params=pltpu.CompilerParams(dimension_semantics=("arbitrary",), vmem_limit_bytes=VMEM_LIMIT),
        name="expert_pairs",
    )(elo, ehi, nused, newcls, buf, wg, wu, wd, wg, wu, wd, g, b)


def _rope_tables(seq):
    half = ROPE_DIM // 2
    inv_freq = jnp.power(jnp.float32(ROPE_THETA), -jnp.arange(half, dtype=f32) / half)
    ang = jnp.arange(seq, dtype=f32)[:, None] * inv_freq[None, :]
    lane = np.arange(LANES) % HEAD_DIM
    fidx = lane % half
    in_rope = lane < ROPE_DIM
    first = lane < half
    cos_l = jnp.cos(ang)[:, fidx]
    sin_l = jnp.sin(ang)[:, fidx]
    cos_t = jnp.where(in_rope[None, :], cos_l, 1.0)
    sa_t = jnp.where((in_rope & first)[None, :], -sin_l, 0.0)
    sb_t = jnp.where((in_rope & ~first)[None, :], sin_l, 0.0)
    return cos_t, sa_t, sb_t


_PAIRS = [(a, b) for a in range(EXPERTS_PER_GROUP) for b in range(a + 1, EXPERTS_PER_GROUP)]


def _dispatch_plan(counts, meta, t):
    counts = counts.astype(jnp.int32)
    padded = ((counts + MOE_BLK - 1) // MOE_BLK) * MOE_BLK
    pend = jnp.cumsum(padded)
    pstart = pend - padded
    nb = (t + N_CLASSES * MOE_BLK) // MOE_BLK
    blk_start = jnp.arange(nb, dtype=jnp.int32) * MOE_BLK
    blk_cls = jnp.minimum(jnp.sum((blk_start[:, None] >= pend[None, :]).astype(jnp.int32), axis=1), N_CLASSES - 1)
    valid_end = pstart[blk_cls] + counts[blk_cls]
    zflag = ((blk_start + MOE_BLK > valid_end) | (blk_start >= pend[-1])).astype(jnp.int32)
    pair_lo = jnp.asarray([p[0] for p in _PAIRS], jnp.int32)
    pair_hi = jnp.asarray([p[1] for p in _PAIRS], jnp.int32)
    grp = blk_cls // N_PAIRS
    elo = grp * EXPERTS_PER_GROUP + pair_lo[blk_cls % N_PAIRS]
    ehi = grp * EXPERTS_PER_GROUP + pair_hi[blk_cls % N_PAIRS]
    nused = (pend[-1] // MOE_BLK).astype(jnp.int32).reshape(1)
    newcls = jnp.concatenate([jnp.ones((1,), jnp.int32), (blk_cls[1:] != blk_cls[:-1]).astype(jnp.int32)])
    cls, rank = meta[:, 0:1, :], meta[:, 1:2, :]
    dest = rank
    for c in range(N_CLASSES):
        dest = dest + jnp.where(cls == c, pstart[c], 0)
    return dest, zflag, elo, ehi, nused, newcls


def kernel(x, w_in, w_out, gmlp_ln_g, gmlp_ln_b, gmlp_w_s, gmlp_b_s, pool_w, pool_scale, ln1_g, ln1_b,
           router_w, router_bias, w_gate, w_up, w_down, ln2_g, ln2_b):
    batch, seq, d = x.shape
    t = batch * seq
    assert d == D_MODEL and seq % ROW_TILE == 0 and t % PERM_CHUNK == 0
    cos_t, sa_t, sb_t = _rope_tables(seq)
    tri = (np.arange(ROW_TILE)[:, None] < np.arange(ROW_TILE)[None, :])
    tri = jnp.asarray(tri, bf16)
    rw_pad = router_w.T.astype(bf16)
    rb = router_bias.reshape(N_EXPERTS, 1)
    n_rows = t + N_CLASSES * MOE_BLK
    dummy_flag = jnp.zeros((1,), jnp.int32)

    xf = x.reshape(t, d)
    for l in range(DEPTH):
        lng = gmlp_ln_g[l].reshape(1, GMLP_WIDTH)
        lnb = gmlp_ln_b[l].reshape(1, GMLP_WIDTH)
        qp, kp, vp, u, vln, pz = _inproj(xf, w_in[l].astype(bf16), cos_t, sa_t, sb_t, lng, lnb, seq)
        attn = _attention(qp, kp, vp, batch, seq)
        bs_full = jnp.repeat(gmlp_b_s[l].T, HEAD_DIM, axis=1)
        wp_bd = jnp.zeros((POOL_WIDTH, POOL_WIDTH), f32)
        for g in range(len(POOL_WINDOWS)):
            wp_bd = wp_bd.at[g * HEAD_DIM:(g + 1) * HEAD_DIM, g * HEAD_DIM:(g + 1) * HEAD_DIM].set(pool_w[l, g])
        wp_bd = wp_bd.astype(bf16)
        gm, pool = _mixer(u, vln, pz, gmlp_w_s[l].astype(bf16), bs_full, wp_bd,
                          pool_scale[l].reshape(1, POOL_WIDTH), seq)
        xe, meta, counts = _outproj(attn, gm, pool, xf, w_out[l].astype(bf16), ln1_g[l].reshape(1, d),
                                    ln1_b[l].reshape(1, d), rw_pad, rb, tri)
        dest, zflag, elo, ehi, nused, newcls = _dispatch_plan(counts[:N_CLASSES, 0], meta, t)
        dest = dest.reshape(t // PERM_CHUNK, 1, PERM_CHUNK)
        buf = _permute(dest, zflag, xe, n_rows, D_MODEL + SIDE, scatter=True)
        ys = _experts(elo, ehi, nused, newcls, buf, w_gate, w_up, w_down,
                      ln2_g[l].reshape(1, d), ln2_b[l].reshape(1, d), l)
        xf = _permute(dest, dummy_flag, ys, t, D_MODEL, scatter=False)
    return xf.reshape(batch, seq, d)
```

```python
import functools

import jax
import jax.numpy as jnp
import numpy as np
from jax import lax
from jax.experimental import pallas as pl
from jax.experimental.pallas import tpu as pltpu

f32 = jnp.float32
bf16 = jnp.bfloat16

D_MODEL = 1024
HEAD_DIM = 64
ATTN_WIDTH = 512
DILATED_CONFIGS = ((128, 1), (512, 4), (2048, 16))
ATTN_HALF = 64
ROPE_THETA = 500000.0
ROPE_DIM = 16
GMLP_WIDTH = 256
GMLP_GROUPS = 4
GMLP_CHUNK = 128
POOL_WIDTH = 256
POOL_WINDOWS = (2, 4, 8, 16)
IN_COLS = 3 * ATTN_WIDTH + 2 * GMLP_WIDTH + POOL_WIDTH
N_EXPERTS = 16
N_EXPERT_GROUPS = 4
EXPERTS_PER_GROUP = 4
EXPERT_FF = 512
DEPTH = 2
DEEPNORM_ALPHA = float((2 * DEPTH) ** 0.25)
LN_EPS = 1e-5
NEG_INF = -1e30

LANES = 128
ROW_TILE = 1024
ATTN_QBLK = 128
POOL_HALO = 32
N_PAIRS = 6
N_CLASSES = N_EXPERT_GROUPS * N_PAIRS
CLASS_ROWS = 32
MOE_BLK = 256
SIDE = LANES
PERM_CHUNK = 2048
PERM_UNROLL = 16
VMEM_LIMIT = 48 * 1024 * 1024


def _layernorm(y, g, b):
    mu = jnp.mean(y, axis=-1, keepdims=True)
    d = y - mu
    var = jnp.mean(d * d, axis=-1, keepdims=True)
    return d * lax.rsqrt(var + LN_EPS) * g + b


def _group_mean(v):
    lane = lax.broadcasted_iota(jnp.int32, v.shape, 1)
    out = jnp.zeros_like(v)
    for g in range(GMLP_GROUPS):
        m = (lane >= g * HEAD_DIM) & (lane < (g + 1) * HEAD_DIM)
        s = jnp.sum(jnp.where(m, v, 0.0), axis=-1, keepdims=True)
        out = jnp.where(m, s, out)
    return out * (1.0 / HEAD_DIM)


def _inproj_kernel(x_ref, w_ref, cos_ref, sa_ref, sb_ref, lng_ref, lnb_ref,
                   q_ref, k_ref, v_ref, u_ref, vln_ref, pz_ref):
    xb = x_ref[...].astype(bf16)
    cos = cos_ref[...]
    sa = sa_ref[...]
    sb = sb_ref[...]
    for base, out_ref, rope, scale in ((0, q_ref, True, HEAD_DIM ** -0.5),
                                       (ATTN_WIDTH, k_ref, True, 1.0),
                                       (2 * ATTN_WIDTH, v_ref, False, 1.0)):
        y = jnp.dot(xb, w_ref[:, base:base + ATTN_WIDTH], preferred_element_type=f32)
        for p in range(ATTN_WIDTH // LANES):
            yp = y[:, p * LANES:(p + 1) * LANES]
            if rope:
                yp = yp * cos + pltpu.roll(yp, LANES - ROPE_DIM // 2, 1) * sa + pltpu.roll(yp, ROPE_DIM // 2, 1) * sb
            if scale != 1.0:
                yp = yp * scale
            out_ref[p] = yp.astype(bf16)
    o3 = 3 * ATTN_WIDTH
    y = jnp.dot(xb, w_ref[:, o3:o3 + 3 * GMLP_WIDTH], preferred_element_type=f32)
    u_ref[...] = jax.nn.gelu(y[:, :GMLP_WIDTH])
    gv = jax.nn.gelu(y[:, GMLP_WIDTH:2 * GMLP_WIDTH])
    mu = _group_mean(gv)
    d = gv - mu
    var = _group_mean(d * d)
    vln_ref[...] = (d * lax.rsqrt(var + LN_EPS) * lng_ref[...] + lnb_ref[...]).astype(bf16)
    pz_ref[...] = y[:, 2 * GMLP_WIDTH:]


def _inproj(x, w_in, cos_t, sa_t, sb_t, lng, lnb, seq):
    t = x.shape[0]
    nt = t // ROW_TILE
    tiles_per_seq = seq // ROW_TILE
    npair = ATTN_WIDTH // LANES
    tab_spec = pl.BlockSpec((ROW_TILE, LANES), lambda i: (i % tiles_per_seq, 0))
    row = lambda w: pl.BlockSpec((ROW_TILE, w), lambda i: (i, 0))
    const = lambda shape: pl.BlockSpec(shape, lambda i: (0,) * len(shape))
    pair_spec = pl.BlockSpec((npair, ROW_TILE, LANES), lambda i: (0, i, 0))
    pair_shape = jax.ShapeDtypeStruct((npair, t, LANES), bf16)
    return pl.pallas_call(
        _inproj_kernel,
        grid=(nt,),
        in_specs=[row(D_MODEL), const((D_MODEL, IN_COLS)), tab_spec, tab_spec, tab_spec,
                  const((1, GMLP_WIDTH)), const((1, GMLP_WIDTH))],
        out_specs=[pair_spec, pair_spec, pair_spec, row(GMLP_WIDTH), row(GMLP_WIDTH), row(POOL_WIDTH)],
        out_shape=[pair_shape, pair_shape, pair_shape,
                   jax.ShapeDtypeStruct((t, GMLP_WIDTH), f32),
                   jax.ShapeDtypeStruct((t, GMLP_WIDTH), bf16),
                   jax.ShapeDtypeStruct((t, POOL_WIDTH), f32)],
        compiler_params=pltpu.CompilerParams(dimension_semantics=("parallel",), vmem_limit_bytes=VMEM_LIMIT),
        name="inproj",
    )(x, w_in, cos_t, sa_t, sb_t, lng, lnb)


def _attn_kernel(q_ref, k_ref, v_ref, bias_ref, o_ref, qs, ks, vs, o0, o1, o2, l0s, l1s, l2s, *, seq):
    os_ = (o0, o1, o2)
    ls_ = (l0s, l1s, l2s)
    qs[...] = q_ref[...].astype(f32)
    ks[...] = k_ref[...].astype(f32)
    vs[...] = v_ref[...].astype(f32)
    lane = lax.broadcasted_iota(jnp.int32, (ATTN_QBLK, LANES), 1)
    head_a = lane < HEAD_DIM

    for ci, (_, dil) in enumerate(DILATED_CONFIGS):
        length = seq // dil
        nblk = length // ATTN_QBLK
        win = min(2 * ATTN_QBLK, length)
        shift = dil.bit_length() - 1

        def rows(start, size, dil=dil):
            if dil == 1:
                return pl.ds(pl.multiple_of(start, 8), size)
            return pl.ds(start, size, stride=dil)

        def body(u, carry, ci=ci, dil=dil, length=length, win=win, shift=shift, rows=rows):
            c = u & (dil - 1)
            i = u >> shift
            k0 = jnp.clip(i * ATTN_QBLK - ATTN_HALF, 0, length - win)
            delta = k0 - i * ATTN_QBLK
            qrows = rows(c + dil * ATTN_QBLK * i, ATTN_QBLK)
            krows = rows(c + dil * k0, win)
            q = qs[qrows, :]
            kb = ks[krows, :].astype(bf16)
            vb = vs[krows, :].astype(bf16)
            bias = bias_ref[lax.shift_right_logical(-delta, ATTN_HALF.bit_length() - 1), :, 0:win]
            q2 = jnp.concatenate([jnp.where(head_a, q, 0.0), jnp.where(head_a, 0.0, q)], axis=0).astype(bf16)
            s = lax.dot_general(q2, kb, (((1,), (1,)), ((), ())), preferred_element_type=f32) + bias
            m = jnp.max(s, axis=-1, keepdims=True)
            p = jnp.exp(s - m).astype(bf16)
            ol = jnp.dot(p, jnp.concatenate([vb, jnp.ones_like(vb)], axis=1), preferred_element_type=f32)
            l = ol[:, LANES:]
            o = ol[:, :LANES] / l
            lse = m + jnp.log(l)
            os_[ci][qrows, :] = jnp.where(head_a, o[:ATTN_QBLK], o[ATTN_QBLK:])
            ls_[ci][qrows, :] = jnp.where(head_a, lse[:ATTN_QBLK], lse[ATTN_QBLK:])
            return carry

        lax.fori_loop(0, dil * nblk, body, 0, unroll=16)

    chunk = 256

    def combine(j, carry):
        r = pl.ds(pl.multiple_of(j * chunk, chunk), chunk)
        l0, l1, l2 = l0s[r, :], l1s[r, :], l2s[r, :]
        mx = jnp.maximum(jnp.maximum(l0, l1), l2)
        w0, w1, w2 = jnp.exp(l0 - mx), jnp.exp(l1 - mx), jnp.exp(l2 - mx)
        num = w0 * o0[r, :] + w1 * o1[r, :] + w2 * o2[r, :]
        o_ref[r, :] = (num / (w0 + w1 + w2)).astype(bf16)
        return carry

    lax.fori_loop(0, seq // chunk, combine, 0)


def _attention(qp, kp, vp, batch, seq):
    npair = qp.shape[0]
    t = batch * seq
    in_spec = pl.BlockSpec((None, seq, LANES), lambda p, b: (p, b, 0))
    win = 2 * ATTN_QBLK
    rr = np.arange(2 * ATTN_QBLK)[None, :, None] % ATTN_QBLK
    cc = np.arange(win)[None, None, :]
    dd = np.arange(3)[:, None, None] * ATTN_HALF
    bias = jnp.asarray(np.where(np.abs(cc - dd - rr) <= ATTN_HALF, 0.0, NEG_INF), f32)
    return pl.pallas_call(
        functools.partial(_attn_kernel, seq=seq),
        grid=(npair, batch),
        in_specs=[in_spec, in_spec, in_spec, pl.BlockSpec(bias.shape, lambda p, b: (0, 0, 0))],
        out_specs=pl.BlockSpec((seq, LANES), lambda p, b: (b, p)),
        out_shape=jax.ShapeDtypeStruct((t, ATTN_WIDTH), bf16),
        scratch_shapes=[pltpu.VMEM((seq, LANES), f32)] * (3 + 2 * len(DILATED_CONFIGS)),
        compiler_params=pltpu.CompilerParams(dimension_semantics=("parallel", "parallel"),
                                             vmem_limit_bytes=VMEM_LIMIT),
        name="dilated_attn",
    )(qp, kp, vp, bias)


def _mixer_kernel(u_ref, vln_ref, pz_ref, prev_ref, next_ref, ws_ref, bs_ref, wp_ref, sc_ref,
                  gm_ref, pool_ref, z0, z1, z2, z3, z4, *, seq):
    tm = ROW_TILE
    h = POOL_HALO
    lane = lax.broadcasted_iota(jnp.int32, (GMLP_CHUNK, GMLP_WIDTH), 1)
    for cc in range(tm // GMLP_CHUNK):
        r = slice(cc * GMLP_CHUNK, (cc + 1) * GMLP_CHUNK)
        vc = vln_ref[r, :]
        sv = jnp.zeros((GMLP_CHUNK, GMLP_WIDTH), f32)
        for g in range(GMLP_GROUPS):
            svg = jnp.dot(ws_ref[g], vc, preferred_element_type=f32)
            sv = jnp.where((lane >= g * HEAD_DIM) & (lane < (g + 1) * HEAD_DIM), svg, sv)
        gm_ref[r, :] = (u_ref[r, :] * (sv + bs_ref[...])).astype(bf16)

    i = pl.program_id(0)
    tiles_per_seq = seq // tm
    pos0 = (i % tiles_per_seq) * tm
    first = pos0 == 0
    last = pos0 + tm == seq
    z0[0:h, :] = jnp.where(first, 0.0, prev_ref[...])
    z0[h:h + tm, :] = pz_ref[...]
    z0[h + tm:h + tm + h, :] = jnp.where(last, 0.0, next_ref[...])
    z1[8:tm + 2 * h - 8, :] = z0[8:tm + 2 * h - 8, :] + z0[7:tm + 2 * h - 9, :]
    z2[16:tm + 2 * h - 16, :] = z1[15:tm + 2 * h - 17, :] + z1[17:tm + 2 * h - 15, :]
    z3[24:tm + 2 * h - 24, :] = z2[22:tm + 2 * h - 26, :] + z2[26:tm + 2 * h - 22, :]
    z4[h:h + tm, :] = z3[h - 4:h + tm - 4, :] + z3[h + 4:h + tm + 4, :]
    lane_t = lax.broadcasted_iota(jnp.int32, (tm, POOL_WIDTH), 1)
    pos = pos0 + lax.broadcasted_iota(jnp.int32, (tm, POOL_WIDTH), 0)
    zc = z0[h:h + tm, :]
    pooled = jnp.zeros((tm, POOL_WIDTH), f32)
    for g, (w, zw) in enumerate(zip(POOL_WINDOWS, (z1, z2, z3, z4))):
        left = w // 2
        right = w - 1 - left
        lo = jnp.maximum(pos - left, 0)
        hi = jnp.minimum(pos + right + 1, seq)
        cnt = (hi - lo).astype(f32)
        val = zw[h:h + tm, :] / cnt - zc
        pooled = jnp.where((lane_t >= g * HEAD_DIM) & (lane_t < (g + 1) * HEAD_DIM), val, pooled)
    y = jnp.dot(pooled.astype(bf16), wp_ref[...], preferred_element_type=f32)
    pool_ref[...] = (y * sc_ref[...]).astype(bf16)


def _mixer(u, vln, pz, ws, bs_full, wp_bd, scale, seq):
    t = u.shape[0]
    tm, h = ROW_TILE, POOL_HALO
    nt = t // tm
    hb = tm // h
    row = lambda w: pl.BlockSpec((tm, w), lambda i: (i, 0))
    const = lambda shape: pl.BlockSpec(shape, lambda i: (0,) * len(shape))
    prev_spec = pl.BlockSpec((h, POOL_WIDTH), lambda i: (jnp.maximum(i * hb - 1, 0), 0))
    next_spec = pl.BlockSpec((h, POOL_WIDTH), lambda i: (jnp.minimum((i + 1) * hb, t // h - 1), 0))
    zbuf = pltpu.VMEM((tm + 2 * h, POOL_WIDTH), f32)
    return pl.pallas_call(
        functools.partial(_mixer_kernel, seq=seq),
        grid=(nt,),
        in_specs=[row(GMLP_WIDTH), row(GMLP_WIDTH), row(POOL_WIDTH), prev_spec, next_spec,
                  const((GMLP_GROUPS, GMLP_CHUNK, GMLP_CHUNK)), const((GMLP_CHUNK, GMLP_WIDTH)),
                  const((POOL_WIDTH, POOL_WIDTH)), const((1, POOL_WIDTH))],
        out_specs=[row(GMLP_WIDTH), row(POOL_WIDTH)],
        out_shape=[jax.ShapeDtypeStruct((t, GMLP_WIDTH), bf16), jax.ShapeDtypeStruct((t, POOL_WIDTH), bf16)],
        scratch_shapes=[zbuf] * 5,
        compiler_params=pltpu.CompilerParams(dimension_semantics=("parallel",), vmem_limit_bytes=VMEM_LIMIT),
        name="mixer",
    )(u, vln, pz, pz, pz, ws, bs_full, wp_bd, scale)


def _top2_of4(b, s):
    v1, i1, s1 = b[0], jnp.zeros(b[0].shape, jnp.int32), s[0]
    for j in range(1, 4):
        gt = b[j] > v1
        v1 = jnp.where(gt, b[j], v1)
        i1 = jnp.where(gt, j, i1)
        s1 = jnp.where(gt, s[j], s1)
    v2 = jnp.full(b[0].shape, -jnp.inf, f32)
    i2 = jnp.full(b[0].shape, -1, jnp.int32)
    s2 = jnp.zeros(b[0].shape, f32)
    for j in range(4):
        cand = jnp.where(i1 != j, b[j], -jnp.inf) > v2
        v2 = jnp.where(cand, b[j], v2)
        i2 = jnp.where(cand, j, i2)
        s2 = jnp.where(cand, s[j], s2)
    return v1 + v2, i1, i2, s1, s2


def _outproj_kernel(attn_ref, gm_ref, pool_ref, x_ref, w_ref, g_ref, b_ref, rw_ref, rb_ref, tri_ref,
                    xe_ref, meta_ref, cnt_ref, carry):
    tm = ROW_TILE

    @pl.when(pl.program_id(0) == 0)
    def _():
        carry[...] = jnp.zeros_like(carry)

    o1, o2 = ATTN_WIDTH, ATTN_WIDTH + GMLP_WIDTH
    hmix = jnp.dot(attn_ref[...], w_ref[0:o1, :], preferred_element_type=f32)
    hmix = hmix + jnp.dot(gm_ref[...], w_ref[o1:o2, :], preferred_element_type=f32)
    hmix = hmix + jnp.dot(pool_ref[...], w_ref[o2:, :], preferred_element_type=f32)
    x1 = _layernorm(DEEPNORM_ALPHA * x_ref[...] + hmix, g_ref[...], b_ref[...])
    xe_ref[:, 0:D_MODEL] = x1

    logits = lax.dot_general(rw_ref[...], x1.astype(bf16), (((1,), (1,)), ((), ())),
                             preferred_element_type=f32)
    scores = 1.0 / (1.0 + jnp.exp(-logits))
    biased = scores + rb_ref[...]
    brow = [biased[e:e + 1, :] for e in range(N_EXPERTS)]
    srow = [scores[e:e + 1, :] for e in range(N_EXPERTS)]
    best = None
    for g in range(N_EXPERT_GROUPS):
        sl = slice(g * EXPERTS_PER_GROUP, (g + 1) * EXPERTS_PER_GROUP)
        gs, i1, i2, s1, s2 = _top2_of4(brow[sl], srow[sl])
        if best is None:
            best = (gs, jnp.zeros(gs.shape, jnp.int32), i1, i2, s1, s2)
        else:
            gt = gs > best[0]
            best = (jnp.where(gt, gs, best[0]), jnp.where(gt, g, best[1]), jnp.where(gt, i1, best[2]),
                    jnp.where(gt, i2, best[3]), jnp.where(gt, s1, best[4]), jnp.where(gt, s2, best[5]))
    _, gsel, i1, i2, s1, s2 = best
    den = s1 + s2
    ga, gb = s1 / den, s2 / den
    first_lo = i1 < i2
    lo = jnp.minimum(i1, i2)
    hi = jnp.maximum(i1, i2)
    g_lo = jnp.where(first_lo, ga, gb)
    g_hi = jnp.where(first_lo, gb, ga)
    pidx = jnp.where(lo == 0, hi - 1, jnp.where(lo == 1, hi + 1, N_PAIRS - 1))
    cls = gsel * N_PAIRS + pidx

    onehot = lax.broadcasted_iota(jnp.int32, (CLASS_ROWS, tm), 0) == cls
    ohf = jnp.where(onehot, 1.0, 0.0)
    prefix = jnp.dot(ohf.astype(bf16), tri_ref[...], preferred_element_type=f32)
    base = carry[:, 0:1]
    rank = jnp.sum(ohf * (prefix + base), axis=0, keepdims=True)
    new_carry = carry[...] + jnp.sum(ohf, axis=1, keepdims=True)
    carry[...] = new_carry
    cnt_ref[...] = new_carry

    srow_id = lax.broadcasted_iota(jnp.int32, (8, tm), 0)
    meta_ref[...] = jnp.where(srow_id == 0, cls, jnp.where(srow_id == 1, rank.astype(jnp.int32), 0))
    side8 = jnp.where(srow_id == 0, g_lo, jnp.where(srow_id == 1, g_hi, 0.0))
    side = jnp.concatenate([side8, jnp.zeros((SIDE - 8, tm), f32)], axis=0)
    xe_ref[:, D_MODEL:] = side.T


def _outproj(attn, gm, pool, x, w_out, g, b, rw_pad, rb, tri):
    t = x.shape[0]
    tm = ROW_TILE
    row = lambda w: pl.BlockSpec((tm, w), lambda i: (i, 0))
    const = lambda shape: pl.BlockSpec(shape, lambda i: (0,) * len(shape))
    return pl.pallas_call(
        _outproj_kernel,
        grid=(t // tm,),
        in_specs=[row(ATTN_WIDTH), row(GMLP_WIDTH), row(POOL_WIDTH), row(D_MODEL), const((D_MODEL, D_MODEL)),
                  const((1, D_MODEL)), const((1, D_MODEL)), const((N_EXPERTS, D_MODEL)), const((N_EXPERTS, 1)),
                  const((tm, tm))],
        out_specs=[row(D_MODEL + SIDE), pl.BlockSpec((None, 8, tm), lambda i: (i, 0, 0)),
                   const((CLASS_ROWS, LANES))],
        out_shape=[jax.ShapeDtypeStruct((t, D_MODEL + SIDE), f32), jax.ShapeDtypeStruct((t // tm, 8, tm), jnp.int32),
                   jax.ShapeDtypeStruct((CLASS_ROWS, LANES), f32)],
        scratch_shapes=[pltpu.VMEM((CLASS_ROWS, LANES), f32)],
        compiler_params=pltpu.CompilerParams(dimension_semantics=("arbitrary",), vmem_limit_bytes=VMEM_LIMIT),
        name="outproj_ln_router",
    )(attn, gm, pool, x, w_out, g, b, rw_pad, rb, tri)


def _permute_kernel(idx_ref, zflag_ref, src_ref, *rest, scatter, nzero):
    if scatter:
        dst_ref, zeros, sem, zsem = rest
    else:
        dst_ref, sem = rest
    step = pl.program_id(0)

    if scatter:
        def zcopy(j):
            return pltpu.make_async_copy(zeros, dst_ref.at[pl.ds(pl.multiple_of(j * MOE_BLK, MOE_BLK), MOE_BLK), :],
                                         zsem)

        @pl.when(step == 0)
        def _():
            zeros[...] = jnp.zeros_like(zeros)

            def zstart(j, c):
                @pl.when(zflag_ref[j] != 0)
                def _():
                    zcopy(j).start()
                return c

            def zwait(j, c):
                @pl.when(zflag_ref[j] != 0)
                def _():
                    zcopy(j).wait()
                return c

            lax.fori_loop(0, nzero, zstart, 0)
            lax.fori_loop(0, nzero, zwait, 0)

    def row_copy(r):
        other = idx_ref[0, 0, r]
        if scatter:
            return pltpu.make_async_copy(src_ref.at[pl.ds(r, 1), :], dst_ref.at[pl.ds(other, 1), :], sem)
        return pltpu.make_async_copy(src_ref.at[pl.ds(other, 1), :], dst_ref.at[pl.ds(r, 1), :], sem)

    def start(g, c):
        for k in range(PERM_UNROLL):
            row_copy(g * PERM_UNROLL + k).start(priority=k % 2)
        return c

    lax.fori_loop(0, PERM_CHUNK // PERM_UNROLL, start, 0)
    if scatter:
        pltpu.make_async_copy(src_ref, dst_ref.at[pl.ds(0, PERM_CHUNK), :], sem).wait()
    else:
        pltpu.make_async_copy(src_ref.at[pl.ds(0, PERM_CHUNK), :], dst_ref, sem).wait()


def _permute(idx, zflag, src, n_dst, width, scatter):
    nsteps = idx.shape[0]
    nzero = zflag.shape[0]
    scratch = [pltpu.SemaphoreType.DMA(())]
    if scatter:
        scratch = [pltpu.VMEM((MOE_BLK, width), f32), pltpu.SemaphoreType.DMA(()), pltpu.SemaphoreType.DMA(())]
    block = pl.BlockSpec((PERM_CHUNK, width), lambda i: (i, 0))
    hbm = pl.BlockSpec(memory_space=pl.ANY)
    smem = pl.BlockSpec(memory_space=pltpu.SMEM)
    return pl.pallas_call(
        functools.partial(_permute_kernel, scatter=scatter, nzero=nzero),
        grid=(nsteps,),
        in_specs=[pl.BlockSpec((1, 1, PERM_CHUNK), lambda i: (i, 0, 0), memory_space=pltpu.SMEM), smem,
                  block if scatter else hbm],
        out_specs=hbm if scatter else block,
        out_shape=jax.ShapeDtypeStruct((n_dst, width), f32),
        scratch_shapes=scratch,
        compiler_params=pltpu.CompilerParams(dimension_semantics=("arbitrary",), vmem_limit_bytes=VMEM_LIMIT),
        name="dispatch_rows" if scatter else "undispatch_rows",
    )(idx, zflag, src)


def _expert_kernel(elo_ref, ehi_ref, nused_ref, new_ref, xb_ref, wg_lo, wu_lo, wd_lo, wg_hi, wu_hi, wd_hi,
                   g_ref, b_ref, out_ref, *scratch):
    wb, pre = scratch[:6], scratch[6]
    j = pl.program_id(0)
    last = pl.num_programs(0) - 1
    jc = jnp.minimum(j, last - 1)

    @pl.when(j == 0)
    def _():
        pre[...] = jnp.zeros_like(pre)

    @pl.when((new_ref[jc] != 0) & (j < last))
    def _():
        for src, dst in zip((wg_lo, wu_lo, wd_lo, wg_hi, wu_hi, wd_hi), wb):
            dst[...] = src[...].astype(bf16)

    @pl.when(j <= nused_ref[0])
    def _():
        out_ref[...] = _layernorm(pre[(j + 1) & 1], g_ref[...], b_ref[...])
        x = xb_ref[:, 0:D_MODEL]
        gates = xb_ref[:, D_MODEL:D_MODEL + SIDE]
        xb = x.astype(bf16)
        moe = jnp.zeros_like(x)
        for col, (wg, wu, wd) in enumerate((wb[0:3], wb[3:6])):
            a = jnp.dot(xb, wg[...], preferred_element_type=f32)
            up = jnp.dot(xb, wu[...], preferred_element_type=f32)
            hidden = (a / (1.0 + jnp.exp(-a))) * up
            y = jnp.dot(hidden.astype(bf16), wd[...], preferred_element_type=f32)
            moe = moe + y * gates[:, col:col + 1]
        pre[j & 1] = DEEPNORM_ALPHA * x + moe

    @pl.when(j > nused_ref[0])
    def _():
        out_ref[...] = jnp.zeros_like(out_ref)


def _experts(elo, ehi, nused, newcls, buf, wg, wu, wd, g, b, layer):
    p = buf.shape[0]
    nb = p // MOE_BLK
    cur = lambda j: jnp.minimum(j, nb - 1)
    wspec = lambda shape, which: pl.BlockSpec(
        (None, None) + shape, (lambda j, elo, ehi, nu, nw: (layer, elo[cur(j)], 0, 0)) if which == 0
        else (lambda j, elo, ehi, nu, nw: (layer, ehi[cur(j)], 0, 0)))
    up_shape, down_shape = (D_MODEL, EXPERT_FF), (EXPERT_FF, D_MODEL)
    const = lambda shape: pl.BlockSpec(shape, lambda j, elo, ehi, nu, nw: (0,) * len(shape))
    grid_spec = pltpu.PrefetchScalarGridSpec(
        num_scalar_prefetch=4,
        grid=(nb + 1,),
        in_specs=[pl.BlockSpec((MOE_BLK, D_MODEL + SIDE), lambda j, elo, ehi, nu, nw: (cur(j), 0)),
                  wspec(up_shape, 0), wspec(up_shape, 0), wspec(down_shape, 0),
                  wspec(up_shape, 1), wspec(up_shape, 1), wspec(down_shape, 1),
                  const((1, D_MODEL)), const((1, D_MODEL))],
        out_specs=pl.BlockSpec((MOE_BLK, D_MODEL), lambda j, elo, ehi, nu, nw: (jnp.maximum(j - 1, 0), 0)),
        scratch_shapes=[pltpu.VMEM(up_shape, bf16), pltpu.VMEM(up_shape, bf16), pltpu.VMEM(down_shape, bf16)] * 2
        + [pltpu.VMEM((2, MOE_BLK, D_MODEL), f32)],
    )
    return pl.pallas_call(
        _expert_kernel,
        grid_spec=grid_spec,
        out_shape=jax.ShapeDtypeStruct((p, D_MODEL), f32),
        compiler_params=pltpu.CompilerParams(dimension_semantics=("arbitrary",), vmem_limit_bytes=VMEM_LIMIT),
        name="expert_pairs",
    )(elo, ehi, nused, newcls, buf, wg, wu, wd, wg, wu, wd, g, b)


def _rope_tables(seq):
    half = ROPE_DIM // 2
    inv_freq = jnp.power(jnp.float32(ROPE_THETA), -jnp.arange(half, dtype=f32) / half)
    ang = jnp.arange(seq, dtype=f32)[:, None] * inv_freq[None, :]
    lane = np.arange(LANES) % HEAD_DIM
    fidx = lane % half
    in_rope = lane < ROPE_DIM
    first = lane < half
    cos_l = jnp.cos(ang)[:, fidx]
    sin_l = jnp.sin(ang)[:, fidx]
    cos_t = jnp.where(in_rope[None, :], cos_l, 1.0)
    sa_t = jnp.where((in_rope & first)[None, :], -sin_l, 0.0)
    sb_t = jnp.where((in_rope & ~first)[None, :], sin_l, 0.0)
    return cos_t, sa_t, sb_t


_PAIRS = [(a, b) for a in range(EXPERTS_PER_GROUP) for b in range(a + 1, EXPERTS_PER_GROUP)]


def _dispatch_plan(counts, meta, t):
    counts = counts.astype(jnp.int32)
    padded = ((counts + MOE_BLK - 1) // MOE_BLK) * MOE_BLK
    pend = jnp.cumsum(padded)
    pstart = pend - padded
    nb = (t + N_CLASSES * MOE_BLK) // MOE_BLK
    blk_start = jnp.arange(nb, dtype=jnp.int32) * MOE_BLK
    blk_cls = jnp.minimum(jnp.sum((blk_start[:, None] >= pend[None, :]).astype(jnp.int32), axis=1), N_CLASSES - 1)
    valid_end = pstart[blk_cls] + counts[blk_cls]
    zflag = ((blk_start + MOE_BLK > valid_end) | (blk_start >= pend[-1])).astype(jnp.int32)
    pair_lo = jnp.asarray([p[0] for p in _PAIRS], jnp.int32)
    pair_hi = jnp.asarray([p[1] for p in _PAIRS], jnp.int32)
    grp = blk_cls // N_PAIRS
    elo = grp * EXPERTS_PER_GROUP + pair_lo[blk_cls % N_PAIRS]
    ehi = grp * EXPERTS_PER_GROUP + pair_hi[blk_cls % N_PAIRS]
    nused = (pend[-1] // MOE_BLK).astype(jnp.int32).reshape(1)
    newcls = jnp.concatenate([jnp.ones((1,), jnp.int32), (blk_cls[1:] != blk_cls[:-1]).astype(jnp.int32)])
    cls, rank = meta[:, 0:1, :], meta[:, 1:2, :]
    dest = rank
    for c in range(N_CLASSES):
        dest = dest + jnp.where(cls == c, pstart[c], 0)
    return dest, zflag, elo, ehi, nused, newcls


def kernel(x, w_in, w_out, gmlp_ln_g, gmlp_ln_b, gmlp_w_s, gmlp_b_s, pool_w, pool_scale, ln1_g, ln1_b,
           router_w, router_bias, w_gate, w_up, w_down, ln2_g, ln2_b):
    batch, seq, d = x.shape
    t = batch * seq
    assert d == D_MODEL and seq % ROW_TILE == 0 and t % PERM_CHUNK == 0
    cos_t, sa_t, sb_t = _rope_tables(seq)
    tri = (np.arange(ROW_TILE)[:, None] < np.arange(ROW_TILE)[None, :])
    tri = jnp.asarray(tri, bf16)
    rw_pad = router_w.T.astype(bf16)
    rb = router_bias.reshape(N_EXPERTS, 1)
    n_rows = t + N_CLASSES * MOE_BLK
    dummy_flag = jnp.zeros((1,), jnp.int32)

    xf = x.reshape(t, d)
    for l in range(DEPTH):
        lng = gmlp_ln_g[l].reshape(1, GMLP_WIDTH)
        lnb = gmlp_ln_b[l].reshape(1, GMLP_WIDTH)
        qp, kp, vp, u, vln, pz = _inproj(xf, w_in[l].astype(bf16), cos_t, sa_t, sb_t, lng, lnb, seq)
        attn = _attention(qp, kp, vp, batch, seq)
        bs_full = jnp.repeat(gmlp_b_s[l].T, HEAD_DIM, axis=1)
        wp_bd = jnp.zeros((POOL_WIDTH, POOL_WIDTH), f32)
        for g in range(len(POOL_WINDOWS)):
            wp_bd = wp_bd.at[g * HEAD_DIM:(g + 1) * HEAD_DIM, g * HEAD_DIM:(g + 1) * HEAD_DIM].set(pool_w[l, g])
        wp_bd = wp_bd.astype(bf16)
        gm, pool = _mixer(u, vln, pz, gmlp_w_s[l].astype(bf16), bs_full, wp_bd,
                          pool_scale[l].reshape(1, POOL_WIDTH), seq)
        xe, meta, counts = _outproj(attn, gm, pool, xf, w_out[l].astype(bf16), ln1_g[l].reshape(1, d),
                                    ln1_b[l].reshape(1, d), rw_pad, rb, tri)
        dest, zflag, elo, ehi, nused, newcls = _dispatch_plan(counts[:N_CLASSES, 0], meta, t)
        dest = dest.reshape(t // PERM_CHUNK, 1, PERM_CHUNK)
        buf = _permute(dest, zflag, xe, n_rows, D_MODEL + SIDE, scatter=True)
        ys = _experts(elo, ehi, nused, newcls, buf, w_gate, w_up, w_down,
                      ln2_g[l].reshape(1, d), ln2_b[l].reshape(1, d), l)
        xf = _permute(dest, dummy_flag, ys, t, D_MODEL, scatter=False)
    return xf.reshape(batch, seq, d)
```

```python
import functools

import jax
import jax.numpy as jnp
import numpy as np
from jax import lax
from jax.experimental import pallas as pl
from jax.experimental.pallas import tpu as pltpu

f32 = jnp.float32
bf16 = jnp.bfloat16

D_MODEL = 1024
HEAD_DIM = 64
ATTN_WIDTH = 512
DILATED_CONFIGS = ((128, 1), (512, 4), (2048, 16))
ATTN_HALF = 64
ROPE_THETA = 500000.0
ROPE_DIM = 16
GMLP_WIDTH = 256
GMLP_GROUPS = 4
GMLP_CHUNK = 128
POOL_WIDTH = 256
POOL_WINDOWS = (2, 4, 8, 16)
IN_COLS = 3 * ATTN_WIDTH + 2 * GMLP_WIDTH + POOL_WIDTH
N_EXPERTS = 16
N_EXPERT_GROUPS = 4
EXPERTS_PER_GROUP = 4
EXPERT_FF = 512
DEPTH = 2
DEEPNORM_ALPHA = float((2 * DEPTH) ** 0.25)
LN_EPS = 1e-5
NEG_INF = -1e30

LANES = 128
ROW_TILE = 1024
ATTN_QBLK = 128
POOL_HALO = 32
N_PAIRS = 6
N_CLASSES = N_EXPERT_GROUPS * N_PAIRS
CLASS_ROWS = 32
MOE_BLK = 256
SIDE = LANES
PERM_CHUNK = 2048
PERM_UNROLL = 16
VMEM_LIMIT = 48 * 1024 * 1024


def _layernorm(y, g, b):
    mu = jnp.mean(y, axis=-1, keepdims=True)
    d = y - mu
    var = jnp.mean(d * d, axis=-1, keepdims=True)
    return d * lax.rsqrt(var + LN_EPS) * g + b


def _group_mean(v):
    lane = lax.broadcasted_iota(jnp.int32, v.shape, 1)
    out = jnp.zeros_like(v)
    for g in range(GMLP_GROUPS):
        m = (lane >= g * HEAD_DIM) & (lane < (g + 1) * HEAD_DIM)
        s = jnp.sum(jnp.where(m, v, 0.0), axis=-1, keepdims=True)
        out = jnp.where(m, s, out)
    return out * (1.0 / HEAD_DIM)


def _inproj_kernel(x_ref, w_ref, cos_ref, sa_ref, sb_ref, lng_ref, lnb_ref,
                   q_ref, k_ref, v_ref, u_ref, vln_ref, pz_ref):
    xb = x_ref[...].astype(bf16)
    cos = cos_ref[...]
    sa = sa_ref[...]
    sb = sb_ref[...]
    for base, out_ref, rope, scale in ((0, q_ref, True, HEAD_DIM ** -0.5),
                                       (ATTN_WIDTH, k_ref, True, 1.0),
                                       (2 * ATTN_WIDTH, v_ref, False, 1.0)):
        y = jnp.dot(xb, w_ref[:, base:base + ATTN_WIDTH], preferred_element_type=f32)
        for p in range(ATTN_WIDTH // LANES):
            yp = y[:, p * LANES:(p + 1) * LANES]
            if rope:
                yp = yp * cos + pltpu.roll(yp, LANES - ROPE_DIM // 2, 1) * sa + pltpu.roll(yp, ROPE_DIM // 2, 1) * sb
            if scale != 1.0:
                yp = yp * scale
            out_ref[p] = yp.astype(bf16)
    o3 = 3 * ATTN_WIDTH
    y = jnp.dot(xb, w_ref[:, o3:o3 + 3 * GMLP_WIDTH], preferred_element_type=f32)
    u_ref[...] = jax.nn.gelu(y[:, :GMLP_WIDTH])
    gv = jax.nn.gelu(y[:, GMLP_WIDTH:2 * GMLP_WIDTH])
    mu = _group_mean(gv)
    d = gv - mu
    var = _group_mean(d * d)
    vln_ref[...] = (d * lax.rsqrt(var + LN_EPS) * lng_ref[...] + lnb_ref[...]).astype(bf16)
    pz_ref[...] = y[:, 2 * GMLP_WIDTH:]


def _inproj(x, w_in, cos_t, sa_t, sb_t, lng, lnb, seq):
    t = x.shape[0]
    nt = t // ROW_TILE
    tiles_per_seq = seq // ROW_TILE
    npair = ATTN_WIDTH // LANES
    tab_spec = pl.BlockSpec((ROW_TILE, LANES), lambda i: (i % tiles_per_seq, 0))
    row = lambda w: pl.BlockSpec((ROW_TILE, w), lambda i: (i, 0))
    const = lambda shape: pl.BlockSpec(shape, lambda i: (0,) * len(shape))
    pair_spec = pl.BlockSpec((npair, ROW_TILE, LANES), lambda i: (0, i, 0))
    pair_shape = jax.ShapeDtypeStruct((npair, t, LANES), bf16)
    return pl.pallas_call(
        _inproj_kernel,
        grid=(nt,),
        in_specs=[row(D_MODEL), const((D_MODEL, IN_COLS)), tab_spec, tab_spec, tab_spec,
                  const((1, GMLP_WIDTH)), const((1, GMLP_WIDTH))],
        out_specs=[pair_spec, pair_spec, pair_spec, row(GMLP_WIDTH), row(GMLP_WIDTH), row(POOL_WIDTH)],
        out_shape=[pair_shape, pair_shape, pair_shape,
                   jax.ShapeDtypeStruct((t, GMLP_WIDTH), f32),
                   jax.ShapeDtypeStruct((t, GMLP_WIDTH), bf16),
                   jax.ShapeDtypeStruct((t, POOL_WIDTH), f32)],
        compiler_params=pltpu.CompilerParams(dimension_semantics=("parallel",), vmem_limit_bytes=VMEM_LIMIT),
        name="inproj",
    )(x, w_in, cos_t, sa_t, sb_t, lng, lnb)


def _attn_kernel(q_ref, k_ref, v_ref, bias_ref, o_ref, qs, ks, vs, o0, o1, o2, l0s, l1s, l2s, *, seq):
    os_ = (o0, o1, o2)
    ls_ = (l0s, l1s, l2s)
    qs[...] = q_ref[...].astype(f32)
    ks[...] = k_ref[...].astype(f32)
    vs[...] = v_ref[...].astype(f32)
    lane = lax.broadcasted_iota(jnp.int32, (ATTN_QBLK, LANES), 1)
    head_a = lane < HEAD_DIM

    for ci, (_, dil) in enumerate(DILATED_CONFIGS):
        length = seq // dil
        nblk = length // ATTN_QBLK
        win = min(2 * ATTN_QBLK, length)
        shift = dil.bit_length() - 1

        def rows(start, size, dil=dil):
            if dil == 1:
                return pl.ds(pl.multiple_of(start, 8), size)
            return pl.ds(start, size, stride=dil)

        def body(u, carry, ci=ci, dil=dil, length=length, win=win, shift=shift, rows=rows):
            c = u & (dil - 1)
            i = u >> shift
            k0 = jnp.clip(i * ATTN_QBLK - ATTN_HALF, 0, length - win)
            delta = k0 - i * ATTN_QBLK
            qrows = rows(c + dil * ATTN_QBLK * i, ATTN_QBLK)
            krows = rows(c + dil * k0, win)
            q = qs[qrows, :]
            kb = ks[krows, :].astype(bf16)
            vb = vs[krows, :].astype(bf16)
            bias = bias_ref[lax.shift_right_logical(-delta, ATTN_HALF.bit_length() - 1), :, 0:win]
            q2 = jnp.concatenate([jnp.where(head_a, q, 0.0), jnp.where(head_a, 0.0, q)], axis=0).astype(bf16)
            s = lax.dot_general(q2, kb, (((1,), (1,)), ((), ())), preferred_element_type=f32) + bias
            m = jnp.max(s, axis=-1, keepdims=True)
            p = jnp.exp(s - m).astype(bf16)
            ol = jnp.dot(p, jnp.concatenate([vb, jnp.ones_like(vb)], axis=1), preferred_element_type=f32)
            l = ol[:, LANES:]
            o = ol[:, :LANES] / l
            lse = m + jnp.log(l)
            os_[ci][qrows, :] = jnp.where(head_a, o[:ATTN_QBLK], o[ATTN_QBLK:])
            ls_[ci][qrows, :] = jnp.where(head_a, lse[:ATTN_QBLK], lse[ATTN_QBLK:])
            return carry

        lax.fori_loop(0, dil * nblk, body, 0, unroll=16)

    chunk = 256

    def combine(j, carry):
        r = pl.ds(pl.multiple_of(j * chunk, chunk), chunk)
        l0, l1, l2 = l0s[r, :], l1s[r, :], l2s[r, :]
        mx = jnp.maximum(jnp.maximum(l0, l1), l2)
        w0, w1, w2 = jnp.exp(l0 - mx), jnp.exp(l1 - mx), jnp.exp(l2 - mx)
        num = w0 * o0[r, :] + w1 * o1[r, :] + w2 * o2[r, :]
        o_ref[r, :] = (num / (w0 + w1 + w2)).astype(bf16)
        return carry

    lax.fori_loop(0, seq // chunk, combine, 0)


def _attention(qp, kp, vp, batch, seq):
    npair = qp.shape[0]
    t = batch * seq
    in_spec = pl.BlockSpec((None, seq, LANES), lambda p, b: (p, b, 0))
    win = 2 * ATTN_QBLK
    rr = np.arange(2 * ATTN_QBLK)[None, :, None] % ATTN_QBLK
    cc = np.arange(win)[None, None, :]
    dd = np.arange(3)[:, None, None] * ATTN_HALF
    bias = jnp.asarray(np.where(np.abs(cc - dd - rr) <= ATTN_HALF, 0.0, NEG_INF), f32)
    return pl.pallas_call(
        functools.partial(_attn_kernel, seq=seq),
        grid=(npair, batch),
        in_specs=[in_spec, in_spec, in_spec, pl.BlockSpec(bias.shape, lambda p, b: (0, 0, 0))],
        out_specs=pl.BlockSpec((seq, LANES), lambda p, b: (b, p)),
        out_shape=jax.ShapeDtypeStruct((t, ATTN_WIDTH), bf16),
        scratch_shapes=[pltpu.VMEM((seq, LANES), f32)] * (3 + 2 * len(DILATED_CONFIGS)),
        compiler_params=pltpu.CompilerParams(dimension_semantics=("parallel", "parallel"),
                                             vmem_limit_bytes=VMEM_LIMIT),
        name="dilated_attn",
    )(qp, kp, vp, bias)


def _mixer_kernel(u_ref, vln_ref, pz_ref, prev_ref, next_ref, ws_ref, bs_ref, wp_ref, sc_ref,
                  gm_ref, pool_ref, z0, z1, z2, z3, z4, *, seq):
    tm = ROW_TILE
    h = POOL_HALO
    lane = lax.broadcasted_iota(jnp.int32, (GMLP_CHUNK, GMLP_WIDTH), 1)
    for cc in range(tm // GMLP_CHUNK):
        r = slice(cc * GMLP_CHUNK, (cc + 1) * GMLP_CHUNK)
        vc = vln_ref[r, :]
        sv = jnp.zeros((GMLP_CHUNK, GMLP_WIDTH), f32)
        for g in range(GMLP_GROUPS):
            svg = jnp.dot(ws_ref[g], vc, preferred_element_type=f32)
            sv = jnp.where((lane >= g * HEAD_DIM) & (lane < (g + 1) * HEAD_DIM), svg, sv)
        gm_ref[r, :] = (u_ref[r, :] * (sv + bs_ref[...])).astype(bf16)

    i = pl.program_id(0)
    tiles_per_seq = seq // tm
    pos0 = (i % tiles_per_seq) * tm
    first = pos0 == 0
    last = pos0 + tm == seq
    z0[0:h, :] = jnp.where(first, 0.0, prev_ref[...])
    z0[h:h + tm, :] = pz_ref[...]
    z0[h + tm:h + tm + h, :] = jnp.where(last, 0.0, next_ref[...])
    z1[8:tm + 2 * h - 8, :] = z0[8:tm + 2 * h - 8, :] + z0[7:tm + 2 * h - 9, :]
    z2[16:tm + 2 * h - 16, :] = z1[15:tm + 2 * h - 17, :] + z1[17:tm + 2 * h - 15, :]
    z3[24:tm + 2 * h - 24, :] = z2[22:tm + 2 * h - 26, :] + z2[26:tm + 2 * h - 22, :]
    z4[h:h + tm, :] = z3[h - 4:h + tm - 4, :] + z3[h + 4:h + tm + 4, :]
    lane_t = lax.broadcasted_iota(jnp.int32, (tm, POOL_WIDTH), 1)
    pos = pos0 + lax.broadcasted_iota(jnp.int32, (tm, POOL_WIDTH), 0)
    zc = z0[h:h + tm, :]
    pooled = jnp.zeros((tm, POOL_WIDTH), f32)
    for g, (w, zw) in enumerate(zip(POOL_WINDOWS, (z1, z2, z3, z4))):
        left = w // 2
        right = w - 1 - left
        lo = jnp.maximum(pos - left, 0)
        hi = jnp.minimum(pos + right + 1, seq)
        cnt = (hi - lo).astype(f32)
        val = zw[h:h + tm, :] / cnt - zc
        pooled = jnp.where((lane_t >= g * HEAD_DIM) & (lane_t < (g + 1) * HEAD_DIM), val, pooled)
    y = jnp.dot(pooled.astype(bf16), wp_ref[...], preferred_element_type=f32)
    pool_ref[...] = (y * sc_ref[...]).astype(bf16)


def _mixer(u, vln, pz, ws, bs_full, wp_bd, scale, seq):
    t = u.shape[0]
    tm, h = ROW_TILE, POOL_HALO
    nt = t // tm
    hb = tm // h
    row = lambda w: pl.BlockSpec((tm, w), lambda i: (i, 0))
    const = lambda shape: pl.BlockSpec(shape, lambda i: (0,) * len(shape))
    prev_spec = pl.BlockSpec((h, POOL_WIDTH), lambda i: (jnp.maximum(i * hb - 1, 0), 0))
    next_spec = pl.BlockSpec((h, POOL_WIDTH), lambda i: (jnp.minimum((i + 1) * hb, t // h - 1), 0))
    zbuf = pltpu.VMEM((tm + 2 * h, POOL_WIDTH), f32)
    return pl.pallas_call(
        functools.partial(_mixer_kernel, seq=seq),
        grid=(nt,),
        in_specs=[row(GMLP_WIDTH), row(GMLP_WIDTH), row(POOL_WIDTH), prev_spec, next_spec,
                  const((GMLP_GROUPS, GMLP_CHUNK, GMLP_CHUNK)), const((GMLP_CHUNK, GMLP_WIDTH)),
                  const((POOL_WIDTH, POOL_WIDTH)), const((1, POOL_WIDTH))],
        out_specs=[row(GMLP_WIDTH), row(POOL_WIDTH)],
        out_shape=[jax.ShapeDtypeStruct((t, GMLP_WIDTH), bf16), jax.ShapeDtypeStruct((t, POOL_WIDTH), bf16)],
        scratch_shapes=[zbuf] * 5,
        compiler_params=pltpu.CompilerParams(dimension_semantics=("parallel",), vmem_limit_bytes=VMEM_LIMIT),
        name="mixer",
    )(u, vln, pz, pz, pz, ws, bs_full, wp_bd, scale)


def _top2_of4(b, s):
    v1, i1, s1 = b[0], jnp.zeros(b[0].shape, jnp.int32), s[0]
    for j in range(1, 4):
        gt = b[j] > v1
        v1 = jnp.where(gt, b[j], v1)
        i1 = jnp.where(gt, j, i1)
        s1 = jnp.where(gt, s[j], s1)
    v2 = jnp.full(b[0].shape, -jnp.inf, f32)
    i2 = jnp.full(b[0].shape, -1, jnp.int32)
    s2 = jnp.zeros(b[0].shape, f32)
    for j in range(4):
        cand = jnp.where(i1 != j, b[j], -jnp.inf) > v2
        v2 = jnp.where(cand, b[j], v2)
        i2 = jnp.where(cand, j, i2)
        s2 = jnp.where(cand, s[j], s2)
    return v1 + v2, i1, i2, s1, s2


def _outproj_kernel(attn_ref, gm_ref, pool_ref, x_ref, w_ref, g_ref, b_ref, rw_ref, rb_ref, tri_ref,
                    xe_ref, meta_ref, cnt_ref, carry):
    tm = ROW_TILE

    @pl.when(pl.program_id(0) == 0)
    def _():
        carry[...] = jnp.zeros_like(carry)

    o1, o2 = ATTN_WIDTH, ATTN_WIDTH + GMLP_WIDTH
    hmix = jnp.dot(attn_ref[...], w_ref[0:o1, :], preferred_element_type=f32)
    hmix = hmix + jnp.dot(gm_ref[...], w_ref[o1:o2, :], preferred_element_type=f32)
    hmix = hmix + jnp.dot(pool_ref[...], w_ref[o2:, :], preferred_element_type=f32)
    x1 = _layernorm(DEEPNORM_ALPHA * x_ref[...] + hmix, g_ref[...], b_ref[...])
    xe_ref[:, 0:D_MODEL] = x1

    logits = lax.dot_general(rw_ref[...], x1.astype(bf16), (((1,), (1,)), ((), ())),
                             preferred_element_type=f32)
    scores = 1.0 / (1.0 + jnp.exp(-logits))
    biased = scores + rb_ref[...]
    brow = [biased[e:e + 1, :] for e in range(N_EXPERTS)]
    srow = [scores[e:e + 1, :] for e in range(N_EXPERTS)]
    best = None
    for g in range(N_EXPERT_GROUPS):
        sl = slice(g * EXPERTS_PER_GROUP, (g + 1) * EXPERTS_PER_GROUP)
        gs, i1, i2, s1, s2 = _top2_of4(brow[sl], srow[sl])
        if best is None:
            best = (gs, jnp.zeros(gs.shape, jnp.int32), i1, i2, s1, s2)
        else:
            gt = gs > best[0]
            best = (jnp.where(gt, gs, best[0]), jnp.where(gt, g, best[1]), jnp.where(gt, i1, best[2]),
                    jnp.where(gt, i2, best[3]), jnp.where(gt, s1, best[4]), jnp.where(gt, s2, best[5]))
    _, gsel, i1, i2, s1, s2 = best
    den = s1 + s2
    ga, gb = s1 / den, s2 / den
    first_lo = i1 < i2
    lo = jnp.minimum(i1, i2)
    hi = jnp.maximum(i1, i2)
    g_lo = jnp.where(first_lo, ga, gb)
    g_hi = jnp.where(first_lo, gb, ga)
    pidx = jnp.where(lo == 0, hi - 1, jnp.where(lo == 1, hi + 1, N_PAIRS - 1))
    cls = gsel * N_PAIRS + pidx

    onehot = lax.broadcasted_iota(jnp.int32, (CLASS_ROWS, tm), 0) == cls
    ohf = jnp.where(onehot, 1.0, 0.0)
    prefix = jnp.dot(ohf.astype(bf16), tri_ref[...], preferred_element_type=f32)
    base = carry[:, 0:1]
    rank = jnp.sum(ohf * (prefix + base), axis=0, keepdims=True)
    new_carry = carry[...] + jnp.sum(ohf, axis=1, keepdims=True)
    carry[...] = new_carry
    cnt_ref[...] = new_carry

    srow_id = lax.broadcasted_iota(jnp.int32, (8, tm), 0)
    meta_ref[...] = jnp.where(srow_id == 0, cls, jnp.where(srow_id == 1, rank.astype(jnp.int32), 0))
    side8 = jnp.where(srow_id == 0, g_lo, jnp.where(srow_id == 1, g_hi, 0.0))
    side = jnp.concatenate([side8, jnp.zeros((SIDE - 8, tm), f32)], axis=0)
    xe_ref[:, D_MODEL:] = side.T


def _outproj(attn, gm, pool, x, w_out, g, b, rw_pad, rb, tri):
    t = x.shape[0]
    tm = ROW_TILE
    row = lambda w: pl.BlockSpec((tm, w), lambda i: (i, 0))
    const = lambda shape: pl.BlockSpec(shape, lambda i: (0,) * len(shape))
    return pl.pallas_call(
        _outproj_kernel,
        grid=(t // tm,),
        in_specs=[row(ATTN_WIDTH), row(GMLP_WIDTH), row(POOL_WIDTH), row(D_MODEL), const((D_MODEL, D_MODEL)),
                  const((1, D_MODEL)), const((1, D_MODEL)), const((N_EXPERTS, D_MODEL)), const((N_EXPERTS, 1)),
                  const((tm, tm))],
        out_specs=[row(D_MODEL + SIDE), pl.BlockSpec((None, 8, tm), lambda i: (i, 0, 0)),
                   const((CLASS_ROWS, LANES))],
        out_shape=[jax.ShapeDtypeStruct((t, D_MODEL + SIDE), f32), jax.ShapeDtypeStruct((t // tm, 8, tm), jnp.int32),
                   jax.ShapeDtypeStruct((CLASS_ROWS, LANES), f32)],
        scratch_shapes=[pltpu.VMEM((CLASS_ROWS, LANES), f32)],
        compiler_params=pltpu.CompilerParams(dimension_semantics=("arbitrary",), vmem_limit_bytes=VMEM_LIMIT),
        name="outproj_ln_router",
    )(attn, gm, pool, x, w_out, g, b, rw_pad, rb, tri)


def _permute_kernel(idx_ref, zflag_ref, src_ref, *rest, scatter, nzero):
    if scatter:
        dst_ref, zeros, sem, zsem = rest
    else:
        dst_ref, sem = rest
    step = pl.program_id(0)

    if scatter:
        def zcopy(j):
            return pltpu.make_async_copy(zeros, dst_ref.at[pl.ds(pl.multiple_of(j * MOE_BLK, MOE_BLK), MOE_BLK), :],
                                         zsem)

        @pl.when(step == 0)
        def _():
            zeros[...] = jnp.zeros_like(zeros)

            def zstart(j, c):
                @pl.when(zflag_ref[j] != 0)
                def _():
                    zcopy(j).start()
                return c

            def zwait(j, c):
                @pl.when(zflag_ref[j] != 0)
                def _():
                    zcopy(j).wait()
                return c

            lax.fori_loop(0, nzero, zstart, 0)
            lax.fori_loop(0, nzero, zwait, 0)

    def row_copy(r):
        other = idx_ref[0, 0, r]
        if scatter:
            return pltpu.make_async_copy(src_ref.at[pl.ds(r, 1), :], dst_ref.at[pl.ds(other, 1), :], sem)
        return pltpu.make_async_copy(src_ref.at[pl.ds(other, 1), :], dst_ref.at[pl.ds(r, 1), :], sem)

    def start(g, c):
        for k in range(PERM_UNROLL):
            row_copy(g * PERM_UNROLL + k).start(priority=k % 2)
        return c

    lax.fori_loop(0, PERM_CHUNK // PERM_UNROLL, start, 0)
    if scatter:
        pltpu.make_async_copy(src_ref, dst_ref.at[pl.ds(0, PERM_CHUNK), :], sem).wait()
    else:
        pltpu.make_async_copy(src_ref.at[pl.ds(0, PERM_CHUNK), :], dst_ref, sem).wait()


def _permute(idx, zflag, src, n_dst, width, scatter):
    nsteps = idx.shape[0]
    nzero = zflag.shape[0]
    scratch = [pltpu.SemaphoreType.DMA(())]
    if scatter:
        scratch = [pltpu.VMEM((MOE_BLK, width), f32), pltpu.SemaphoreType.DMA(()), pltpu.SemaphoreType.DMA(())]
    block = pl.BlockSpec((PERM_CHUNK, width), lambda i: (i, 0))
    hbm = pl.BlockSpec(memory_space=pl.ANY)
    smem = pl.BlockSpec(memory_space=pltpu.SMEM)
    return pl.pallas_call(
        functools.partial(_permute_kernel, scatter=scatter, nzero=nzero),
        grid=(nsteps,),
        in_specs=[pl.BlockSpec((1, 1, PERM_CHUNK), lambda i: (i, 0, 0), memory_space=pltpu.SMEM), smem,
                  block if scatter else hbm],
        out_specs=hbm if scatter else block,
        out_shape=jax.ShapeDtypeStruct((n_dst, width), f32),
        scratch_shapes=scratch,
        compiler_params=pltpu.CompilerParams(dimension_semantics=("arbitrary",), vmem_limit_bytes=VMEM_LIMIT),
        name="dispatch_rows" if scatter else "undispatch_rows",
    )(idx, zflag, src)


def _expert_kernel(elo_ref, ehi_ref, nused_ref, new_ref, xb_ref, wg_lo, wu_lo, wd_lo, wg_hi, wu_hi, wd_hi,
                   g_ref, b_ref, out_ref, *scratch):
    wb, pre = scratch[:6], scratch[6]
    j = pl.program_id(0)
    last = pl.num_programs(0) - 1
    jc = jnp.minimum(j, last - 1)

    @pl.when(j == 0)
    def _():
        pre[...] = jnp.zeros_like(pre)

    @pl.when((new_ref[jc] != 0) & (j < last))
    def _():
        for src, dst in zip((wg_lo, wu_lo, wd_lo, wg_hi, wu_hi, wd_hi), wb):
            dst[...] = src[...].astype(bf16)

    @pl.when(j <= nused_ref[0])
    def _():
        out_ref[...] = _layernorm(pre[(j + 1) & 1], g_ref[...], b_ref[...])
        x = xb_ref[:, 0:D_MODEL]
        gates = xb_ref[:, D_MODEL:D_MODEL + SIDE]
        xb = x.astype(bf16)
        moe = jnp.zeros_like(x)
        for col, (wg, wu, wd) in enumerate((wb[0:3], wb[3:6])):
            a = jnp.dot(xb, wg[...], preferred_element_type=f32)
            up = jnp.dot(xb, wu[...], preferred_element_type=f32)
            hidden = (a / (1.0 + jnp.exp(-a))) * up
            y = jnp.dot(hidden.astype(bf16), wd[...], preferred_element_type=f32)
            moe = moe + y * gates[:, col:col + 1]
        pre[j & 1] = DEEPNORM_ALPHA * x + moe

    @pl.when(j > nused_ref[0])
    def _():
        out_ref[...] = jnp.zeros_like(out_ref)


def _experts(elo, ehi, nused, newcls, buf, wg, wu, wd, g, b, layer):
    p = buf.shape[0]
    nb = p // MOE_BLK
    cur = lambda j: jnp.minimum(j, nb - 1)
    wspec = lambda shape, which: pl.BlockSpec(
        (None, None) + shape, (lambda j, elo, ehi, nu, nw: (layer, elo[cur(j)], 0, 0)) if which == 0
        else (lambda j, elo, ehi, nu, nw: (layer, ehi[cur(j)], 0, 0)))
    up_shape, down_shape = (D_MODEL, EXPERT_FF), (EXPERT_FF, D_MODEL)
    const = lambda shape: pl.BlockSpec(shape, lambda j, elo, ehi, nu, nw: (0,) * len(shape))
    grid_spec = pltpu.PrefetchScalarGridSpec(
        num_scalar_prefetch=4,
        grid=(nb + 1,),
        in_specs=[pl.BlockSpec((MOE_BLK, D_MODEL + SIDE), lambda j, elo, ehi, nu, nw: (cur(j), 0)),
                  wspec(up_shape, 0), wspec(up_shape, 0), wspec(down_shape, 0),
                  wspec(up_shape, 1), wspec(up_shape, 1), wspec(down_shape, 1),
                  const((1, D_MODEL)), const((1, D_MODEL))],
        out_specs=pl.BlockSpec((MOE_BLK, D_MODEL), lambda j, elo, ehi, nu, nw: (jnp.maximum(j - 1, 0), 0)),
        scratch_shapes=[pltpu.VMEM(up_shape, bf16), pltpu.VMEM(up_shape, bf16), pltpu.VMEM(down_shape, bf16)] * 2
        + [pltpu.VMEM((2, MOE_BLK, D_MODEL), f32)],
    )
    return pl.pallas_call(
        _expert_kernel,
        grid_spec=grid_spec,
        out_shape=jax.ShapeDtypeStruct((p, D_MODEL), f32),
        compiler_params=pltpu.CompilerParams(dimension_semantics=("arbitrary",), vmem_limit_bytes=VMEM_LIMIT),
        name="expert_pairs",
    )(elo, ehi, nused, newcls, buf, wg, wu, wd, wg, wu, wd, g, b)


def _rope_tables(seq):
    half = ROPE_DIM // 2
    inv_freq = np.power(np.float64(ROPE_THETA), -np.arange(half, dtype=np.float64) / half)
    ang = np.arange(seq, dtype=np.float64)[:, None] * inv_freq[None, :]
    lane = np.arange(LANES) % HEAD_DIM
    fidx = lane % half
    in_rope = lane < ROPE_DIM
    first = lane < half
    cos_l = np.cos(ang)[:, fidx]
    sin_l = np.sin(ang)[:, fidx]
    cos_t = np.where(in_rope[None, :], cos_l, 1.0)
    sa_t = np.where((in_rope & first)[None, :], -sin_l, 0.0)
    sb_t = np.where((in_rope & ~first)[None, :], sin_l, 0.0)
    return jnp.asarray(cos_t, f32), jnp.asarray(sa_t, f32), jnp.asarray(sb_t, f32)


_PAIRS = [(a, b) for a in range(EXPERTS_PER_GROUP) for b in range(a + 1, EXPERTS_PER_GROUP)]


def _dispatch_plan(counts, meta, t):
    counts = counts.astype(jnp.int32)
    padded = ((counts + MOE_BLK - 1) // MOE_BLK) * MOE_BLK
    ci = np.arange(N_CLASSES)
    pend = jnp.sum(jnp.where(ci[None, :] <= ci[:, None], padded[None, :], 0), axis=1)
    pstart = pend - padded
    nb = (t + N_CLASSES * MOE_BLK) // MOE_BLK
    blk_start = jnp.arange(nb, dtype=jnp.int32) * MOE_BLK
    blk_cls = jnp.minimum(jnp.sum((blk_start[:, None] >= pend[None, :]).astype(jnp.int32), axis=1), N_CLASSES - 1)
    valid_end = pstart[blk_cls] + counts[blk_cls]
    zflag = ((blk_start + MOE_BLK > valid_end) | (blk_start >= pend[-1])).astype(jnp.int32)
    pair_lo = jnp.asarray([p[0] for p in _PAIRS], jnp.int32)
    pair_hi = jnp.asarray([p[1] for p in _PAIRS], jnp.int32)
    grp = blk_cls // N_PAIRS
    elo = grp * EXPERTS_PER_GROUP + pair_lo[blk_cls % N_PAIRS]
    ehi = grp * EXPERTS_PER_GROUP + pair_hi[blk_cls % N_PAIRS]
    nused = (pend[-1] // MOE_BLK).astype(jnp.int32).reshape(1)
    newcls = jnp.concatenate([jnp.ones((1,), jnp.int32), (blk_cls[1:] != blk_cls[:-1]).astype(jnp.int32)])
    cls, rank = meta[:, 0:1, :], meta[:, 1:2, :]
    dest = rank
    for c in range(N_CLASSES):
        dest = dest + jnp.where(cls == c, pstart[c], 0)
    return dest, zflag, elo, ehi, nused, newcls


def kernel(x, w_in, w_out, gmlp_ln_g, gmlp_ln_b, gmlp_w_s, gmlp_b_s, pool_w, pool_scale, ln1_g, ln1_b,
           router_w, router_bias, w_gate, w_up, w_down, ln2_g, ln2_b):
    batch, seq, d = x.shape
    t = batch * seq
    assert d == D_MODEL and seq % ROW_TILE == 0 and t % PERM_CHUNK == 0
    cos_t, sa_t, sb_t = _rope_tables(seq)
    tri = (np.arange(ROW_TILE)[:, None] < np.arange(ROW_TILE)[None, :])
    tri = jnp.asarray(tri, bf16)
    rw_pad = router_w.T.astype(bf16)
    rb = router_bias.reshape(N_EXPERTS, 1)
    n_rows = t + N_CLASSES * MOE_BLK
    dummy_flag = jnp.zeros((1,), jnp.int32)

    xf = x.reshape(t, d)
    for l in range(DEPTH):
        lng = gmlp_ln_g[l].reshape(1, GMLP_WIDTH)
        lnb = gmlp_ln_b[l].reshape(1, GMLP_WIDTH)
        qp, kp, vp, u, vln, pz = _inproj(xf, w_in[l].astype(bf16), cos_t, sa_t, sb_t, lng, lnb, seq)
        attn = _attention(qp, kp, vp, batch, seq)
        bs_full = jnp.repeat(gmlp_b_s[l].T, HEAD_DIM, axis=1)
        eye = np.eye(len(POOL_WINDOWS), dtype=np.float32)
        wp_bd = (eye[:, None, :, None] * pool_w[l][:, :, None, :]).reshape(POOL_WIDTH, POOL_WIDTH).astype(bf16)
        gm, pool = _mixer(u, vln, pz, gmlp_w_s[l].astype(bf16), bs_full, wp_bd,
                          pool_scale[l].reshape(1, POOL_WIDTH), seq)
        xe, meta, counts = _outproj(attn, gm, pool, xf, w_out[l].astype(bf16), ln1_g[l].reshape(1, d),
                                    ln1_b[l].reshape(1, d), rw_pad, rb, tri)
        dest, zflag, elo, ehi, nused, newcls = _dispatch_plan(counts[:N_CLASSES, 0], meta, t)
        dest = dest.reshape(t // PERM_CHUNK, 1, PERM_CHUNK)
        buf = _permute(dest, zflag, xe, n_rows, D_MODEL + SIDE, scatter=True)
        ys = _experts(elo, ehi, nused, newcls, buf, w_gate, w_up, w_down,
                      ln2_g[l].reshape(1, d), ln2_b[l].reshape(1, d), l)
        xf = _permute(dest, dummy_flag, ys, t, D_MODEL, scatter=False)
    return xf.reshape(batch, seq, d)
```

```python
import functools

import jax
import jax.numpy as jnp
import numpy as np
from jax import lax
from jax.experimental import pallas as pl
from jax.experimental.pallas import tpu as pltpu

f32 = jnp.float32
bf16 = jnp.bfloat16

D_MODEL = 1024
HEAD_DIM = 64
ATTN_WIDTH = 512
DILATED_CONFIGS = ((128, 1), (512, 4), (2048, 16))
ATTN_HALF = 64
ROPE_THETA = 500000.0
ROPE_DIM = 16
GMLP_WIDTH = 256
GMLP_GROUPS = 4
GMLP_CHUNK = 128
POOL_WIDTH = 256
POOL_WINDOWS = (2, 4, 8, 16)
IN_COLS = 3 * ATTN_WIDTH + 2 * GMLP_WIDTH + POOL_WIDTH
N_EXPERTS = 16
N_EXPERT_GROUPS = 4
EXPERTS_PER_GROUP = 4
EXPERT_FF = 512
DEPTH = 2
DEEPNORM_ALPHA = float((2 * DEPTH) ** 0.25)
LN_EPS = 1e-5
NEG_INF = -1e30

LANES = 128
ROW_TILE = 1024
ATTN_QBLK = 128
POOL_HALO = 32
N_PAIRS = 6
N_CLASSES = N_EXPERT_GROUPS * N_PAIRS
CLASS_ROWS = 32
MOE_BLK = 256
SIDE = LANES
PERM_CHUNK = 2048
PERM_UNROLL = 16
VMEM_LIMIT = 48 * 1024 * 1024


def _layernorm(y, g, b):
    mu = jnp.mean(y, axis=-1, keepdims=True)
    d = y - mu
    var = jnp.mean(d * d, axis=-1, keepdims=True)
    return d * lax.rsqrt(var + LN_EPS) * g + b


def _group_mean(v):
    lane = lax.broadcasted_iota(jnp.int32, v.shape, 1)
    out = jnp.zeros_like(v)
    for g in range(GMLP_GROUPS):
        m = (lane >= g * HEAD_DIM) & (lane < (g + 1) * HEAD_DIM)
        s = jnp.sum(jnp.where(m, v, 0.0), axis=-1, keepdims=True)
        out = jnp.where(m, s, out)
    return out * (1.0 / HEAD_DIM)


def _inproj_kernel(x_ref, w_ref, cos_ref, sa_ref, sb_ref, lng_ref, lnb_ref,
                   q_ref, k_ref, v_ref, u_ref, vln_ref, pz_ref):
    xb = x_ref[...].astype(bf16)
    cos = cos_ref[...]
    sa = sa_ref[...]
    sb = sb_ref[...]
    for base, out_ref, rope, scale in ((0, q_ref, True, HEAD_DIM ** -0.5),
                                       (ATTN_WIDTH, k_ref, True, 1.0),
                                       (2 * ATTN_WIDTH, v_ref, False, 1.0)):
        y = jnp.dot(xb, w_ref[:, base:base + ATTN_WIDTH], preferred_element_type=f32)
        for p in range(ATTN_WIDTH // LANES):
            yp = y[:, p * LANES:(p + 1) * LANES]
            if rope:
                yp = yp * cos + pltpu.roll(yp, LANES - ROPE_DIM // 2, 1) * sa + pltpu.roll(yp, ROPE_DIM // 2, 1) * sb
            if scale != 1.0:
                yp = yp * scale
            out_ref[p] = yp.astype(bf16)
    o3 = 3 * ATTN_WIDTH
    y = jnp.dot(xb, w_ref[:, o3:o3 + 3 * GMLP_WIDTH], preferred_element_type=f32)
    u_ref[...] = jax.nn.gelu(y[:, :GMLP_WIDTH])
    gv = jax.nn.gelu(y[:, GMLP_WIDTH:2 * GMLP_WIDTH])
    mu = _group_mean(gv)
    d = gv - mu
    var = _group_mean(d * d)
    vln_ref[...] = (d * lax.rsqrt(var + LN_EPS) * lng_ref[...] + lnb_ref[...]).astype(bf16)
    pz_ref[...] = y[:, 2 * GMLP_WIDTH:]


def _inproj(x, w_in, cos_t, sa_t, sb_t, lng, lnb, seq):
    t = x.shape[0]
    nt = t // ROW_TILE
    tiles_per_seq = seq // ROW_TILE
    npair = ATTN_WIDTH // LANES
    tab_spec = pl.BlockSpec((ROW_TILE, LANES), lambda i: (i % tiles_per_seq, 0))
    row = lambda w: pl.BlockSpec((ROW_TILE, w), lambda i: (i, 0))
    const = lambda shape: pl.BlockSpec(shape, lambda i: (0,) * len(shape))
    pair_spec = pl.BlockSpec((npair, ROW_TILE, LANES), lambda i: (0, i, 0))
    pair_shape = jax.ShapeDtypeStruct((npair, t, LANES), bf16)
    return pl.pallas_call(
        _inproj_kernel,
        grid=(nt,),
        in_specs=[row(D_MODEL), const((D_MODEL, IN_COLS)), tab_spec, tab_spec, tab_spec,
                  const((1, GMLP_WIDTH)), const((1, GMLP_WIDTH))],
        out_specs=[pair_spec, pair_spec, pair_spec, row(GMLP_WIDTH), row(GMLP_WIDTH), row(POOL_WIDTH)],
        out_shape=[pair_shape, pair_shape, pair_shape,
                   jax.ShapeDtypeStruct((t, GMLP_WIDTH), f32),
                   jax.ShapeDtypeStruct((t, GMLP_WIDTH), bf16),
                   jax.ShapeDtypeStruct((t, POOL_WIDTH), f32)],
        compiler_params=pltpu.CompilerParams(dimension_semantics=("parallel",), vmem_limit_bytes=VMEM_LIMIT),
        name="inproj",
    )(x, w_in, cos_t, sa_t, sb_t, lng, lnb)


def _attn_kernel(q_ref, k_ref, v_ref, bias_ref, o_ref, qs, ks, vs, o0, o1, o2, l0s, l1s, l2s, *, seq):
    os_ = (o0, o1, o2)
    ls_ = (l0s, l1s, l2s)
    qs[...] = q_ref[...].astype(f32)
    ks[...] = k_ref[...].astype(f32)
    vs[...] = v_ref[...].astype(f32)
    lane = lax.broadcasted_iota(jnp.int32, (ATTN_QBLK, LANES), 1)
    head_a = lane < HEAD_DIM

    for ci, (_, dil) in enumerate(DILATED_CONFIGS):
        length = seq // dil
        nblk = length // ATTN_QBLK
        win = min(2 * ATTN_QBLK, length)
        shift = dil.bit_length() - 1

        def rows(start, size, dil=dil):
            if dil == 1:
                return pl.ds(pl.multiple_of(start, 8), size)
            return pl.ds(start, size, stride=dil)

        def body(u, carry, ci=ci, dil=dil, length=length, win=win, shift=shift, rows=rows):
            c = u & (dil - 1)
            i = u >> shift
            k0 = jnp.clip(i * ATTN_QBLK - ATTN_HALF, 0, length - win)
            delta = k0 - i * ATTN_QBLK
            qrows = rows(c + dil * ATTN_QBLK * i, ATTN_QBLK)
            krows = rows(c + dil * k0, win)
            q = qs[qrows, :]
            kb = ks[krows, :].astype(bf16)
            vb = vs[krows, :].astype(bf16)
            bias = bias_ref[lax.shift_right_logical(-delta, ATTN_HALF.bit_length() - 1), :, 0:win]
            q2 = jnp.concatenate([jnp.where(head_a, q, 0.0), jnp.where(head_a, 0.0, q)], axis=0).astype(bf16)
            s = lax.dot_general(q2, kb, (((1,), (1,)), ((), ())), preferred_element_type=f32) + bias
            m = jnp.max(s, axis=-1, keepdims=True)
            p = jnp.exp(s - m).astype(bf16)
            ol = jnp.dot(p, jnp.concatenate([vb, jnp.ones_like(vb)], axis=1), preferred_element_type=f32)
            l = ol[:, LANES:]
            o = ol[:, :LANES] / l
            lse = m + jnp.log(l)
            os_[ci][qrows, :] = jnp.where(head_a, o[:ATTN_QBLK], o[ATTN_QBLK:])
            ls_[ci][qrows, :] = jnp.where(head_a, lse[:ATTN_QBLK], lse[ATTN_QBLK:])
            return carry

        lax.fori_loop(0, dil * nblk, body, 0, unroll=16)

    chunk = 256

    def combine(j, carry):
        r = pl.ds(pl.multiple_of(j * chunk, chunk), chunk)
        l0, l1, l2 = l0s[r, :], l1s[r, :], l2s[r, :]
        mx = jnp.maximum(jnp.maximum(l0, l1), l2)
        w0, w1, w2 = jnp.exp(l0 - mx), jnp.exp(l1 - mx), jnp.exp(l2 - mx)
        num = w0 * o0[r, :] + w1 * o1[r, :] + w2 * o2[r, :]
        o_ref[r, :] = (num / (w0 + w1 + w2)).astype(bf16)
        return carry

    lax.fori_loop(0, seq // chunk, combine, 0)


def _attention(qp, kp, vp, batch, seq):
    npair = qp.shape[0]
    t = batch * seq
    in_spec = pl.BlockSpec((None, seq, LANES), lambda p, b: (p, b, 0))
    win = 2 * ATTN_QBLK
    rr = np.arange(2 * ATTN_QBLK)[None, :, None] % ATTN_QBLK
    cc = np.arange(win)[None, None, :]
    dd = np.arange(3)[:, None, None] * ATTN_HALF
    bias = jnp.asarray(np.where(np.abs(cc - dd - rr) <= ATTN_HALF, 0.0, NEG_INF), f32)
    return pl.pallas_call(
        functools.partial(_attn_kernel, seq=seq),
        grid=(npair, batch),
        in_specs=[in_spec, in_spec, in_spec, pl.BlockSpec(bias.shape, lambda p, b: (0, 0, 0))],
        out_specs=pl.BlockSpec((seq, LANES), lambda p, b: (b, p)),
        out_shape=jax.ShapeDtypeStruct((t, ATTN_WIDTH), bf16),
        scratch_shapes=[pltpu.VMEM((seq, LANES), f32)] * (3 + 2 * len(DILATED_CONFIGS)),
        compiler_params=pltpu.CompilerParams(dimension_semantics=("parallel", "parallel"),
                                             vmem_limit_bytes=VMEM_LIMIT),
        name="dilated_attn",
    )(qp, kp, vp, bias)


def _mixer_kernel(u_ref, vln_ref, pz_ref, prev_ref, next_ref, ws_ref, bs_ref, wp_ref, sc_ref,
                  gm_ref, pool_ref, z0, z1, z2, z3, z4, *, seq):
    tm = ROW_TILE
    h = POOL_HALO
    lane = lax.broadcasted_iota(jnp.int32, (GMLP_CHUNK, GMLP_WIDTH), 1)
    for cc in range(tm // GMLP_CHUNK):
        r = slice(cc * GMLP_CHUNK, (cc + 1) * GMLP_CHUNK)
        vc = vln_ref[r, :]
        sv = jnp.zeros((GMLP_CHUNK, GMLP_WIDTH), f32)
        for g in range(GMLP_GROUPS):
            svg = jnp.dot(ws_ref[g], vc, preferred_element_type=f32)
            sv = jnp.where((lane >= g * HEAD_DIM) & (lane < (g + 1) * HEAD_DIM), svg, sv)
        gm_ref[r, :] = (u_ref[r, :] * (sv + bs_ref[...])).astype(bf16)

    i = pl.program_id(0)
    tiles_per_seq = seq // tm
    pos0 = (i % tiles_per_seq) * tm
    first = pos0 == 0
    last = pos0 + tm == seq
    z0[0:h, :] = jnp.where(first, 0.0, prev_ref[...])
    z0[h:h + tm, :] = pz_ref[...]
    z0[h + tm:h + tm + h, :] = jnp.where(last, 0.0, next_ref[...])
    z1[8:tm + 2 * h - 8, :] = z0[8:tm + 2 * h - 8, :] + z0[7:tm + 2 * h - 9, :]
    z2[16:tm + 2 * h - 16, :] = z1[15:tm + 2 * h - 17, :] + z1[17:tm + 2 * h - 15, :]
    z3[24:tm + 2 * h - 24, :] = z2[22:tm + 2 * h - 26, :] + z2[26:tm + 2 * h - 22, :]
    z4[h:h + tm, :] = z3[h - 4:h + tm - 4, :] + z3[h + 4:h + tm + 4, :]
    lane_t = lax.broadcasted_iota(jnp.int32, (tm, POOL_WIDTH), 1)
    pos = pos0 + lax.broadcasted_iota(jnp.int32, (tm, POOL_WIDTH), 0)
    zc = z0[h:h + tm, :]
    pooled = jnp.zeros((tm, POOL_WIDTH), f32)
    for g, (w, zw) in enumerate(zip(POOL_WINDOWS, (z1, z2, z3, z4))):
        left = w // 2
        right = w - 1 - left
        lo = jnp.maximum(pos - left, 0)
        hi = jnp.minimum(pos + right + 1, seq)
        cnt = (hi - lo).astype(f32)
        val = zw[h:h + tm, :] / cnt - zc
        pooled = jnp.where((lane_t >= g * HEAD_DIM) & (lane_t < (g + 1) * HEAD_DIM), val, pooled)
    y = jnp.dot(pooled.astype(bf16), wp_ref[...], preferred_element_type=f32)
    pool_ref[...] = (y * sc_ref[...]).astype(bf16)


def _top2_of4(b, s):
    v1, i1, s1 = b[0], jnp.zeros(b[0].shape, jnp.int32), s[0]
    for j in range(1, 4):
        gt = b[j] > v1
        v1 = jnp.where(gt, b[j], v1)
        i1 = jnp.where(gt, j, i1)
        s1 = jnp.where(gt, s[j], s1)
    v2 = jnp.full(b[0].shape, -jnp.inf, f32)
    i2 = jnp.full(b[0].shape, -1, jnp.int32)
    s2 = jnp.zeros(b[0].shape, f32)
    for j in range(4):
        cand = jnp.where(i1 != j, b[j], -jnp.inf) > v2
        v2 = jnp.where(cand, b[j], v2)
        i2 = jnp.where(cand, j, i2)
        s2 = jnp.where(cand, s[j], s2)
    return v1 + v2, i1, i2, s1, s2


def _outproj_kernel(attn_ref, u_ref, vln_ref, pz_ref, prev_ref, next_ref, ws_ref, bs_ref, wp_ref, sc_ref,
                    x_ref, w_ref, g_ref, b_ref, rw_ref, rb_ref, tri_ref,
                    xe_ref, meta_ref, cnt_ref, carry, gm_ref, pool_ref, z0, z1, z2, z3, z4, *, seq):
    tm = ROW_TILE
    _mixer_kernel(u_ref, vln_ref, pz_ref, prev_ref, next_ref, ws_ref, bs_ref, wp_ref, sc_ref,
                  gm_ref, pool_ref, z0, z1, z2, z3, z4, seq=seq)

    @pl.when(pl.program_id(0) == 0)
    def _():
        carry[...] = jnp.zeros_like(carry)

    o1, o2 = ATTN_WIDTH, ATTN_WIDTH + GMLP_WIDTH
    hmix = jnp.dot(attn_ref[...], w_ref[0:o1, :], preferred_element_type=f32)
    hmix = hmix + jnp.dot(gm_ref[...], w_ref[o1:o2, :], preferred_element_type=f32)
    hmix = hmix + jnp.dot(pool_ref[...], w_ref[o2:, :], preferred_element_type=f32)
    x1 = _layernorm(DEEPNORM_ALPHA * x_ref[...] + hmix, g_ref[...], b_ref[...])
    xe_ref[:, 0:D_MODEL] = x1

    logits = lax.dot_general(rw_ref[...], x1.astype(bf16), (((1,), (1,)), ((), ())),
                             preferred_element_type=f32)
    scores = 1.0 / (1.0 + jnp.exp(-logits))
    biased = scores + rb_ref[...]
    brow = [biased[e:e + 1, :] for e in range(N_EXPERTS)]
    srow = [scores[e:e + 1, :] for e in range(N_EXPERTS)]
    best = None
    for g in range(N_EXPERT_GROUPS):
        sl = slice(g * EXPERTS_PER_GROUP, (g + 1) * EXPERTS_PER_GROUP)
        gs, i1, i2, s1, s2 = _top2_of4(brow[sl], srow[sl])
        if best is None:
            best = (gs, jnp.zeros(gs.shape, jnp.int32), i1, i2, s1, s2)
        else:
            gt = gs > best[0]
            best = (jnp.where(gt, gs, best[0]), jnp.where(gt, g, best[1]), jnp.where(gt, i1, best[2]),
                    jnp.where(gt, i2, best[3]), jnp.where(gt, s1, best[4]), jnp.where(gt, s2, best[5]))
    _, gsel, i1, i2, s1, s2 = best
    den = s1 + s2
    ga, gb = s1 / den, s2 / den
    first_lo = i1 < i2
    lo = jnp.minimum(i1, i2)
    hi = jnp.maximum(i1, i2)
    g_lo = jnp.where(first_lo, ga, gb)
    g_hi = jnp.where(first_lo, gb, ga)
    pidx = jnp.where(lo == 0, hi - 1, jnp.where(lo == 1, hi + 1, N_PAIRS - 1))
    cls = gsel * N_PAIRS + pidx

    onehot = lax.broadcasted_iota(jnp.int32, (CLASS_ROWS, tm), 0) == cls
    ohf = jnp.where(onehot, 1.0, 0.0)
    prefix = jnp.dot(ohf.astype(bf16), tri_ref[...], preferred_element_type=f32)
    base = carry[:, 0:1]
    rank = jnp.sum(ohf * (prefix + base), axis=0, keepdims=True)
    new_carry = carry[...] + jnp.sum(ohf, axis=1, keepdims=True)
    carry[...] = new_carry
    cnt_ref[...] = new_carry

    srow_id = lax.broadcasted_iota(jnp.int32, (8, tm), 0)
    meta_ref[...] = jnp.where(srow_id == 0, cls, jnp.where(srow_id == 1, rank.astype(jnp.int32), 0))
    side8 = jnp.where(srow_id == 0, g_lo, jnp.where(srow_id == 1, g_hi, 0.0))
    side = jnp.concatenate([side8, jnp.zeros((SIDE - 8, tm), f32)], axis=0)
    xe_ref[:, D_MODEL:] = side.T


def _outproj(attn, u, vln, pz, ws, bs_full, wp_bd, scale, x, w_out, g, b, rw_pad, rb, tri, seq):
    t = x.shape[0]
    tm, h = ROW_TILE, POOL_HALO
    hb = tm // h
    row = lambda w: pl.BlockSpec((tm, w), lambda i: (i, 0))
    const = lambda shape: pl.BlockSpec(shape, lambda i: (0,) * len(shape))
    prev_spec = pl.BlockSpec((h, POOL_WIDTH), lambda i: (jnp.maximum(i * hb - 1, 0), 0))
    next_spec = pl.BlockSpec((h, POOL_WIDTH), lambda i: (jnp.minimum((i + 1) * hb, t // h - 1), 0))
    zbuf = pltpu.VMEM((tm + 2 * h, POOL_WIDTH), f32)
    return pl.pallas_call(
        functools.partial(_outproj_kernel, seq=seq),
        grid=(t // tm,),
        in_specs=[row(ATTN_WIDTH), row(GMLP_WIDTH), row(GMLP_WIDTH), row(POOL_WIDTH), prev_spec, next_spec,
                  const((GMLP_GROUPS, GMLP_CHUNK, GMLP_CHUNK)), const((GMLP_CHUNK, GMLP_WIDTH)),
                  const((POOL_WIDTH, POOL_WIDTH)), const((1, POOL_WIDTH)),
                  row(D_MODEL), const((D_MODEL, D_MODEL)),
                  const((1, D_MODEL)), const((1, D_MODEL)), const((N_EXPERTS, D_MODEL)), const((N_EXPERTS, 1)),
                  const((tm, tm))],
        out_specs=[row(D_MODEL + SIDE), pl.BlockSpec((None, 8, tm), lambda i: (i, 0, 0)),
                   const((CLASS_ROWS, LANES))],
        out_shape=[jax.ShapeDtypeStruct((t, D_MODEL + SIDE), f32), jax.ShapeDtypeStruct((t // tm, 8, tm), jnp.int32),
                   jax.ShapeDtypeStruct((CLASS_ROWS, LANES), f32)],
        scratch_shapes=[pltpu.VMEM((CLASS_ROWS, LANES), f32), pltpu.VMEM((tm, GMLP_WIDTH), bf16),
                        pltpu.VMEM((tm, POOL_WIDTH), bf16)] + [zbuf] * 5,
        compiler_params=pltpu.CompilerParams(dimension_semantics=("arbitrary",), vmem_limit_bytes=VMEM_LIMIT),
        name="mixer_outproj_ln_router",
    )(attn, u, vln, pz, pz, pz, ws, bs_full, wp_bd, scale, x, w_out, g, b, rw_pad, rb, tri)


def _permute_kernel(idx_ref, zflag_ref, src_ref, *rest, scatter, nzero):
    if scatter:
        dst_ref, zeros, sem, zsem = rest
    else:
        dst_ref, sem = rest
    step = pl.program_id(0)

    if scatter:
        def zcopy(j):
            return pltpu.make_async_copy(zeros, dst_ref.at[pl.ds(pl.multiple_of(j * MOE_BLK, MOE_BLK), MOE_BLK), :],
                                         zsem)

        @pl.when(step == 0)
        def _():
            zeros[...] = jnp.zeros_like(zeros)

            def zstart(j, c):
                @pl.when(zflag_ref[j] != 0)
                def _():
                    zcopy(j).start()
                return c

            def zwait(j, c):
                @pl.when(zflag_ref[j] != 0)
                def _():
                    zcopy(j).wait()
                return c

            lax.fori_loop(0, nzero, zstart, 0)
            lax.fori_loop(0, nzero, zwait, 0)

    def row_copy(r):
        other = idx_ref[0, 0, r]
        if scatter:
            return pltpu.make_async_copy(src_ref.at[pl.ds(r, 1), :], dst_ref.at[pl.ds(other, 1), :], sem)
        return pltpu.make_async_copy(src_ref.at[pl.ds(other, 1), :], dst_ref.at[pl.ds(r, 1), :], sem)

    def start(g, c):
        for k in range(PERM_UNROLL):
            row_copy(g * PERM_UNROLL + k).start(priority=k % 2)
        return c

    lax.fori_loop(0, PERM_CHUNK // PERM_UNROLL, start, 0)
    if scatter:
        pltpu.make_async_copy(src_ref, dst_ref.at[pl.ds(0, PERM_CHUNK), :], sem).wait()
    else:
        pltpu.make_async_copy(src_ref.at[pl.ds(0, PERM_CHUNK), :], dst_ref, sem).wait()


def _permute(idx, zflag, src, n_dst, width, scatter):
    nsteps = idx.shape[0]
    nzero = zflag.shape[0]
    scratch = [pltpu.SemaphoreType.DMA(())]
    if scatter:
        scratch = [pltpu.VMEM((MOE_BLK, width), f32), pltpu.SemaphoreType.DMA(()), pltpu.SemaphoreType.DMA(())]
    block = pl.BlockSpec((PERM_CHUNK, width), lambda i: (i, 0))
    hbm = pl.BlockSpec(memory_space=pl.ANY)
    smem = pl.BlockSpec(memory_space=pltpu.SMEM)
    return pl.pallas_call(
        functools.partial(_permute_kernel, scatter=scatter, nzero=nzero),
        grid=(nsteps,),
        in_specs=[pl.BlockSpec((1, 1, PERM_CHUNK), lambda i: (i, 0, 0), memory_space=pltpu.SMEM), smem,
                  block if scatter else hbm],
        out_specs=hbm if scatter else block,
        out_shape=jax.ShapeDtypeStruct((n_dst, width), f32),
        scratch_shapes=scratch,
        compiler_params=pltpu.CompilerParams(dimension_semantics=("arbitrary",), vmem_limit_bytes=VMEM_LIMIT),
        name="dispatch_rows" if scatter else "undispatch_rows",
    )(idx, zflag, src)


def _expert_kernel(elo_ref, ehi_ref, nused_ref, new_ref, xb_ref, wg_lo, wu_lo, wd_lo, wg_hi, wu_hi, wd_hi,
                   g_ref, b_ref, out_ref, *scratch):
    wb, pre = scratch[:6], scratch[6]
    j = pl.program_id(0)
    last = pl.num_programs(0) - 1
    jc = jnp.minimum(j, last - 1)

    @pl.when(j == 0)
    def _():
        pre[...] = jnp.zeros_like(pre)

    @pl.when((new_ref[jc] != 0) & (j < last))
    def _():
        for src, dst in zip((wg_lo, wu_lo, wd_lo, wg_hi, wu_hi, wd_hi), wb):
            dst[...] = src[...].astype(bf16)

    @pl.when(j <= nused_ref[0])
    def _():
        out_ref[...] = _layernorm(pre[(j + 1) & 1], g_ref[...], b_ref[...])
        x = xb_ref[:, 0:D_MODEL]
        gates = xb_ref[:, D_MODEL:D_MODEL + SIDE]
        xb = x.astype(bf16)
        moe = jnp.zeros_like(x)
        for col, (wg, wu, wd) in enumerate((wb[0:3], wb[3:6])):
            a = jnp.dot(xb, wg[...], preferred_element_type=f32)
            up = jnp.dot(xb, wu[...], preferred_element_type=f32)
            hidden = (a / (1.0 + jnp.exp(-a))) * up
            y = jnp.dot(hidden.astype(bf16), wd[...], preferred_element_type=f32)
            moe = moe + y * gates[:, col:col + 1]
        pre[j & 1] = DEEPNORM_ALPHA * x + moe

    @pl.when(j > nused_ref[0])
    def _():
        out_ref[...] = jnp.zeros_like(out_ref)


def _experts(elo, ehi, nused, newcls, buf, wg, wu, wd, g, b, layer):
    p = buf.shape[0]
    nb = p // MOE_BLK
    cur = lambda j: jnp.minimum(j, nb - 1)
    wspec = lambda shape, which: pl.BlockSpec(
        (None, None) + shape, (lambda j, elo, ehi, nu, nw: (layer, elo[cur(j)], 0, 0)) if which == 0
        else (lambda j, elo, ehi, nu, nw: (layer, ehi[cur(j)], 0, 0)))
    up_shape, down_shape = (D_MODEL, EXPERT_FF), (EXPERT_FF, D_MODEL)
    const = lambda shape: pl.BlockSpec(shape, lambda j, elo, ehi, nu, nw: (0,) * len(shape))
    grid_spec = pltpu.PrefetchScalarGridSpec(
        num_scalar_prefetch=4,
        grid=(nb + 1,),
        in_specs=[pl.BlockSpec((MOE_BLK, D_MODEL + SIDE), lambda j, elo, ehi, nu, nw: (cur(j), 0)),
                  wspec(up_shape, 0), wspec(up_shape, 0), wspec(down_shape, 0),
                  wspec(up_shape, 1), wspec(up_shape, 1), wspec(down_shape, 1),
                  const((1, D_MODEL)), const((1, D_MODEL))],
        out_specs=pl.BlockSpec((MOE_BLK, D_MODEL), lambda j, elo, ehi, nu, nw: (jnp.maximum(j - 1, 0), 0)),
        scratch_shapes=[pltpu.VMEM(up_shape, bf16), pltpu.VMEM(up_shape, bf16), pltpu.VMEM(down_shape, bf16)] * 2
        + [pltpu.VMEM((2, MOE_BLK, D_MODEL), f32)],
    )
    return pl.pallas_call(
        _expert_kernel,
        grid_spec=grid_spec,
        out_shape=jax.ShapeDtypeStruct((p, D_MODEL), f32),
        compiler_params=pltpu.CompilerParams(dimension_semantics=("arbitrary",), vmem_limit_bytes=VMEM_LIMIT),
        name="expert_pairs",
    )(elo, ehi, nused, newcls, buf, wg, wu, wd, wg, wu, wd, g, b)


def _rope_tables(seq):
    half = ROPE_DIM // 2
    inv_freq = np.power(np.float64(ROPE_THETA), -np.arange(half, dtype=np.float64) / half)
    ang = np.arange(seq, dtype=np.float64)[:, None] * inv_freq[None, :]
    lane = np.arange(LANES) % HEAD_DIM
    fidx = lane % half
    in_rope = lane < ROPE_DIM
    first = lane < half
    cos_l = np.cos(ang)[:, fidx]
    sin_l = np.sin(ang)[:, fidx]
    cos_t = np.where(in_rope[None, :], cos_l, 1.0)
    sa_t = np.where((in_rope & first)[None, :], -sin_l, 0.0)
    sb_t = np.where((in_rope & ~first)[None, :], sin_l, 0.0)
    return jnp.asarray(cos_t, f32), jnp.asarray(sa_t, f32), jnp.asarray(sb_t, f32)


_PAIRS = [(a, b) for a in range(EXPERTS_PER_GROUP) for b in range(a + 1, EXPERTS_PER_GROUP)]


def _dispatch_plan(counts, meta, t):
    counts = counts.astype(jnp.int32)
    padded = ((counts + MOE_BLK - 1) // MOE_BLK) * MOE_BLK
    ci = np.arange(N_CLASSES)
    pend = jnp.sum(jnp.where(ci[None, :] <= ci[:, None], padded[None, :], 0), axis=1)
    pstart = pend - padded
    nb = (t + N_CLASSES * MOE_BLK) // MOE_BLK
    blk_start = jnp.arange(nb, dtype=jnp.int32) * MOE_BLK
    blk_cls = jnp.minimum(jnp.sum((blk_start[:, None] >= pend[None, :]).astype(jnp.int32), axis=1), N_CLASSES - 1)
    valid_end = pstart[blk_cls] + counts[blk_cls]
    zflag = ((blk_start + MOE_BLK > valid_end) | (blk_start >= pend[-1])).astype(jnp.int32)
    pair_lo = jnp.asarray([p[0] for p in _PAIRS], jnp.int32)
    pair_hi = jnp.asarray([p[1] for p in _PAIRS], jnp.int32)
    grp = blk_cls // N_PAIRS
    elo = grp * EXPERTS_PER_GROUP + pair_lo[blk_cls % N_PAIRS]
    ehi = grp * EXPERTS_PER_GROUP + pair_hi[blk_cls % N_PAIRS]
    nused = (pend[-1] // MOE_BLK).astype(jnp.int32).reshape(1)
    newcls = jnp.concatenate([jnp.ones((1,), jnp.int32), (blk_cls[1:] != blk_cls[:-1]).astype(jnp.int32)])
    cls, rank = meta[:, 0:1, :], meta[:, 1:2, :]
    dest = rank
    for c in range(N_CLASSES):
        dest = dest + jnp.where(cls == c, pstart[c], 0)
    return dest, zflag, elo, ehi, nused, newcls


def kernel(x, w_in, w_out, gmlp_ln_g, gmlp_ln_b, gmlp_w_s, gmlp_b_s, pool_w, pool_scale, ln1_g, ln1_b,
           router_w, router_bias, w_gate, w_up, w_down, ln2_g, ln2_b):
    batch, seq, d = x.shape
    t = batch * seq
    assert d == D_MODEL and seq % ROW_TILE == 0 and t % PERM_CHUNK == 0
    cos_t, sa_t, sb_t = _rope_tables(seq)
    tri = (np.arange(ROW_TILE)[:, None] < np.arange(ROW_TILE)[None, :])
    tri = jnp.asarray(tri, bf16)
    rw_pad = router_w.T.astype(bf16)
    rb = router_bias.reshape(N_EXPERTS, 1)
    n_rows = t + N_CLASSES * MOE_BLK
    dummy_flag = jnp.zeros((1,), jnp.int32)

    xf = x.reshape(t, d)
    for l in range(DEPTH):
        lng = gmlp_ln_g[l].reshape(1, GMLP_WIDTH)
        lnb = gmlp_ln_b[l].reshape(1, GMLP_WIDTH)
        qp, kp, vp, u, vln, pz = _inproj(xf, w_in[l].astype(bf16), cos_t, sa_t, sb_t, lng, lnb, seq)
        attn = _attention(qp, kp, vp, batch, seq)
        bs_full = jnp.repeat(gmlp_b_s[l].T, HEAD_DIM, axis=1)
        eye = np.eye(len(POOL_WINDOWS), dtype=np.float32)
        wp_bd = (eye[:, None, :, None] * pool_w[l][:, :, None, :]).reshape(POOL_WIDTH, POOL_WIDTH).astype(bf16)
        xe, meta, counts = _outproj(attn, u, vln, pz, gmlp_w_s[l].astype(bf16), bs_full, wp_bd,
                                    pool_scale[l].reshape(1, POOL_WIDTH), xf, w_out[l].astype(bf16),
                                    ln1_g[l].reshape(1, d), ln1_b[l].reshape(1, d), rw_pad, rb, tri, seq)
        dest, zflag, elo, ehi, nused, newcls = _dispatch_plan(counts[:N_CLASSES, 0], meta, t)
        dest = dest.reshape(t // PERM_CHUNK, 1, PERM_CHUNK)
        buf = _permute(dest, zflag, xe, n_rows, D_MODEL + SIDE, scatter=True)
        ys = _experts(elo, ehi, nused, newcls, buf, w_gate, w_up, w_down,
                      ln2_g[l].reshape(1, d), ln2_b[l].reshape(1, d), l)
        xf = _permute(dest, dummy_flag, ys, t, D_MODEL, scatter=False)
    return xf.reshape(batch, seq, d)
```
